```python
import math, functools
import jax, jax.numpy as jnp
from jax import lax
import numpy as np

D_MODEL = 4096
BATCH = 4
SEQ = 2048
DEPTH = 2
DEC_BATCH = 128
DEC_SEQ = 8
PAST_LEN = 16384
PAGE_SIZE = 128

N_META = 16
CHUNK = 64
CONV_W = 4
EPS = 1e-6
N_EVEN = (DEPTH + 1) // 2
N_ODD = DEPTH // 2

SSD_HEAD_DIM = 64
SSD_D_INNER = D_MODEL
SSD_HEADS = SSD_D_INNER // SSD_HEAD_DIM
SSD_STATE = 128
SSD_GROUPS = 8
SSD_CONV_DIM = SSD_D_INNER + 2 * SSD_GROUPS * SSD_STATE

MLSTM_HEADS = 8
MLSTM_DV = D_MODEL // MLSTM_HEADS
MLSTM_DK = MLSTM_DV // 2
MLSTM_QK = MLSTM_HEADS * MLSTM_DK
MLSTM_V = MLSTM_HEADS * MLSTM_DV

EVEN_SPLITS = (SSD_D_INNER, SSD_CONV_DIM, SSD_HEADS, 2 * MLSTM_QK, MLSTM_V, 2 * MLSTM_HEADS, MLSTM_V)
EVEN_IN = sum(EVEN_SPLITS)
EVEN_MIX = SSD_D_INNER + MLSTM_V

GLA_HEADS = 4
GLA_DK = D_MODEL // 2 // GLA_HEADS
GLA_DV = D_MODEL // GLA_HEADS
GLA_RANK = 16
GLA_TAU = 16.0
ODD_SPLITS = (GLA_HEADS * GLA_DK, GLA_HEADS * GLA_DK, GLA_HEADS * GLA_DV, GLA_HEADS * GLA_DV, GLA_RANK)
ODD_IN = sum(ODD_SPLITS)
ODD_MIX = GLA_HEADS * GLA_DV

D_FF = 11008
N_EXPERTS = 8
TOP_K = 2
D_FF_E = 14336

kernel_name = "hybrid_ssd_mlstm_gla_moe_step"

F32 = jnp.float32


def split_cols(p, sizes):
    idx = np.cumsum(sizes)[:-1].tolist()
    return jnp.split(p, idx, axis=-1)


def rmsnorm(x, g):
    x32 = x.astype(F32)
    y = x32 * lax.rsqrt(jnp.mean(x32 * x32, axis=-1, keepdims=True) + EPS)
    return (y * g.astype(F32)).astype(x.dtype)


def group_rmsnorm(x, g, n_groups):
    x32 = x.astype(F32)
    shp = x32.shape
    xg = x32.reshape(shp[:-1] + (n_groups, shp[-1] // n_groups))
    xg = xg * lax.rsqrt(jnp.mean(xg * xg, axis=-1, keepdims=True) + EPS)
    return xg.reshape(shp) * g.astype(F32)


def causal_conv(u, buf, w, b):
    L = u.shape[1]
    full = jnp.concatenate([buf.astype(u.dtype), u], axis=1)
    out = b
    for k in range(CONV_W):
        out = out + full[:, k:k + L] * w[k]
    return out, full[:, L:].astype(buf.dtype)


def to_chunks(a, q):
    b, L = a.shape[:2]
    return a.reshape((b, L // q, q) + a.shape[2:]).swapaxes(0, 1)


def from_chunks(y):
    nc, b, q = y.shape[:3]
    return y.swapaxes(0, 1).reshape((b, nc * q) + y.shape[3:])


def ssd_scan(x, dt, bm, cm, h0, q, A):
    tril = jnp.tril(jnp.ones((q, q), dtype=bool))

    def step(h, inp):
        xc, dtc, bc, cc = inp
        cum = jnp.cumsum(dtc * A, axis=1)
        seg = cum[:, :, None] - cum[:, None, :]
        lmat = jnp.exp(jnp.where(tril[None, :, :, None, None], seg, -jnp.inf))
        cb = jnp.einsum('bign,bjgn->bijg', cc, bc)
        y_intra = jnp.einsum('bijg,bijgh,bjgh,bjghp->bighp', cb, lmat, dtc, xc)
        y_inter = jnp.einsum('bign,bghpn->bighp', cc, h) * jnp.exp(cum)[..., None]
        w_end = jnp.exp(cum[:, -1:] - cum) * dtc
        h_new = jnp.exp(cum[:, -1])[..., None, None] * h + jnp.einsum('bjgh,bjgn,bjghp->bghpn', w_end, bc, xc)
        return h_new, y_intra + y_inter

    hT, ys = lax.scan(step, h0, (to_chunks(x, q), to_chunks(dt, q), to_chunks(bm, q), to_chunks(cm, q)))
    return from_chunks(ys), hT


def mlstm_scan(qh, kh, vh, log_i, log_f, state, q):
    tril = jnp.tril(jnp.ones((q, q), dtype=bool))

    def step(carry, inp):
        cmat, nvec, m = carry
        qc, kc, vc, ic, fc = inp
        cumf = jnp.cumsum(fc, axis=1)
        dmat = cumf[:, :, None] - cumf[:, None, :] + ic[:, None, :]
        dmat = jnp.where(tril[None, :, :, None], dmat, -jnp.inf)
        from_state = cumf + m[:, None]
        m_row = jnp.maximum(jnp.max(dmat, axis=2), from_state)
        w = jnp.exp(dmat - m_row[:, :, None])
        s_state = jnp.exp(from_state - m_row)
        qk = jnp.einsum('bihd,bjhd->bijh', qc, kc) * w
        num = jnp.einsum('bijh,bjhv->bihv', qk, vc) + jnp.einsum('bihd,bhdv->bihv', qc, cmat) * s_state[..., None]
        den = jnp.sum(qk, axis=2) + jnp.einsum('bihd,bhd->bih', qc, nvec) * s_state
        den = jnp.maximum(jnp.abs(den), jnp.exp(-m_row))
        h = num / den[..., None]
        m_new = m_row[:, -1]
        w_end = jnp.exp(cumf[:, -1:] - cumf + ic - m_new[:, None])
        s_end = jnp.exp(cumf[:, -1] + m - m_new)
        c_new = s_end[..., None, None] * cmat + jnp.einsum('bjh,bjhd,bjhv->bhdv', w_end, kc, vc)
        n_new = s_end[..., None] * nvec + jnp.einsum('bjh,bjhd->bhd', w_end, kc)
        return (c_new, n_new, m_new), h

    st, hs = lax.scan(step, state, (to_chunks(qh, q), to_chunks(kh, q), to_chunks(vh, q),
                                    to_chunks(log_i, q), to_chunks(log_f, q)))
    return from_chunks(hs), st


def gla_scan(qh, kh, vh, log_a, s0, q):
    tril = jnp.tril(jnp.ones((q, q), dtype=bool))

    def step(s, inp):
        qc, kc, vc, ac = inp
        bcum = jnp.cumsum(ac, axis=1)
        qt = qc * jnp.exp(bcum)
        kt = kc * jnp.exp(-bcum)
        att = jnp.where(tril[None, None], jnp.einsum('bihd,bjhd->bhij', qt, kt), 0.0)
        o = jnp.einsum('bhij,bjhv->bihv', att, vc) + jnp.einsum('bihd,bhdv->bihv', qt, s)
        b_last = bcum[:, -1]
        s_new = jnp.exp(b_last)[..., None] * s + jnp.einsum('bjhd,bjhv->bhdv', kc * jnp.exp(b_last[:, None] - bcum), vc)
        return s_new, o

    sT, os_ = lax.scan(step, s0, (to_chunks(qh, q), to_chunks(kh, q), to_chunks(vh, q), to_chunks(log_a, q)))
    return from_chunks(os_), sT


def run_segments(scan_fn, seq_inputs, state, n_lead):
    L = seq_inputs[0].shape[1]
    segs = [(0, n_lead), (n_lead, L)] if n_lead > 0 else [(0, L)]
    outs = []
    for s, e in segs:
        q = math.gcd(e - s, CHUNK)
        y, state = scan_fn(*[a[:, s:e] for a in seq_inputs], state, q)
        outs.append(y)
    return jnp.concatenate(outs, axis=1), state


def even_mixer(xn, st_ssd_conv, st_ssd, st_ml_conv, st_ml_c, st_ml_n, st_ml_m, n_lead,
               w_in, ssd_conv_w, ssd_conv_b, ssd_dt_bias, ssd_a_log, ssd_d, ssd_norm,
               mlstm_conv_w, mlstm_conv_b, mlstm_if_bias, mlstm_norm, w_out):
    bsz, L, _ = xn.shape
    hpg = SSD_HEADS // SSD_GROUPS
    z, xbc, dt_raw, qk, v, if_raw, o_raw = split_cols(xn @ w_in, EVEN_SPLITS)
    xbc, new_ssd_conv = causal_conv(xbc, st_ssd_conv, ssd_conv_w, ssd_conv_b)
    xbc = jax.nn.silu(xbc.astype(F32))
    xs, bm, cm = split_cols(xbc, (SSD_D_INNER, SSD_GROUPS * SSD_STATE, SSD_GROUPS * SSD_STATE))
    xs = xs.reshape(bsz, L, SSD_GROUPS, hpg, SSD_HEAD_DIM)
    bm = bm.reshape(bsz, L, SSD_GROUPS, SSD_STATE)
    cm = cm.reshape(bsz, L, SSD_GROUPS, SSD_STATE)
    dt = jax.nn.softplus(dt_raw.astype(F32) + ssd_dt_bias.astype(F32)).reshape(bsz, L, SSD_GROUPS, hpg)
    A = -jnp.exp(ssd_a_log.astype(F32)).reshape(SSD_GROUPS, hpg)
    h0 = st_ssd.astype(F32).reshape(bsz, SSD_GROUPS, hpg, SSD_HEAD_DIM, SSD_STATE)
    y_ssd, h_new = run_segments(functools.partial(ssd_scan, A=A), [xs, dt, bm, cm], h0, n_lead)
    y_ssd = y_ssd + ssd_d.astype(F32).reshape(SSD_GROUPS, hpg)[..., None] * xs
    y_ssd = y_ssd.reshape(bsz, L, SSD_D_INNER) * jax.nn.silu(z.astype(F32))
    y_ssd = group_rmsnorm(y_ssd, ssd_norm, SSD_GROUPS)
    qk, new_ml_conv = causal_conv(qk, st_ml_conv, mlstm_conv_w, mlstm_conv_b)
    qk = jax.nn.silu(qk.astype(F32))
    qm, km = split_cols(qk, (MLSTM_QK, MLSTM_QK))
    qm = qm.reshape(bsz, L, MLSTM_HEADS, MLSTM_DK) * (MLSTM_DK ** -0.5)
    km = km.reshape(bsz, L, MLSTM_HEADS, MLSTM_DK)
    vm = v.astype(F32).reshape(bsz, L, MLSTM_HEADS, MLSTM_DV)
    gates = if_raw.astype(F32) + mlstm_if_bias.astype(F32)
    log_i, f_pre = split_cols(gates, (MLSTM_HEADS, MLSTM_HEADS))
    log_f = jax.nn.log_sigmoid(f_pre)
    ml_state = (st_ml_c.astype(F32), st_ml_n.astype(F32), st_ml_m.astype(F32))
    h_ml, (c_new, n_new, m_new) = run_segments(mlstm_scan, [qm, km, vm, log_i, log_f], ml_state, n_lead)
    h_ml = group_rmsnorm(h_ml.reshape(bsz, L, MLSTM_V), mlstm_norm, MLSTM_HEADS) * jax.nn.sigmoid(o_raw.astype(F32))
    out = jnp.concatenate([y_ssd, h_ml], axis=-1).astype(xn.dtype) @ w_out
    new_states = (new_ssd_conv,
                  h_new.reshape(bsz, SSD_HEADS, SSD_HEAD_DIM, SSD_STATE).astype(st_ssd.dtype),
                  new_ml_conv,
                  c_new.astype(st_ml_c.dtype), n_new.astype(st_ml_n.dtype), m_new.astype(st_ml_m.dtype))
    return out, new_states


def odd_mixer(xn, st_gla, n_lead, w_in, gla_w_gate_up, gla_b_gate, gla_norm, w_out):
    bsz, L, _ = xn.shape
    q, k, v, r, ga = split_cols(xn @ w_in, ODD_SPLITS)
    qh = q.astype(F32).reshape(bsz, L, GLA_HEADS, GLA_DK) * (GLA_DK ** -0.5)
    kh = k.astype(F32).reshape(bsz, L, GLA_HEADS, GLA_DK)
    vh = v.astype(F32).reshape(bsz, L, GLA_HEADS, GLA_DV)
    gate_pre = (ga @ gla_w_gate_up).astype(F32) + gla_b_gate.astype(F32)
    log_a = (jax.nn.log_sigmoid(gate_pre) / GLA_TAU).reshape(bsz, L, GLA_HEADS, GLA_DK)
    o, s_new = run_segments(gla_scan, [qh, kh, vh, log_a], st_gla.astype(F32), n_lead)
    o = group_rmsnorm(o.reshape(bsz, L, ODD_MIX), gla_norm, GLA_HEADS) * jax.nn.silu(r.astype(F32))
    return o.astype(xn.dtype) @ w_out, s_new.astype(st_gla.dtype)


def swiglu(x, wg, wu, wd):
    return (jax.nn.silu(x @ wg) * (x @ wu)) @ wd


def moe_swiglu(x, w_router, w1, w3, w2):
    logits = (x @ w_router).astype(F32)
    top_v, top_i = lax.top_k(logits, TOP_K)
    gates = jax.nn.softmax(top_v, axis=-1)
    combine = jnp.sum(jax.nn.one_hot(top_i, N_EXPERTS, dtype=F32) * gates[..., None], axis=-2)
    y = jnp.zeros(x.shape, F32)
    for e in range(N_EXPERTS):
        y = y + combine[..., e:e + 1] * swiglu(x, w1[e], w3[e], w2[e]).astype(F32)
    return y.astype(x.dtype)


def run_trunk(x, st_ssd_conv, st_ssd, st_ml_conv, st_ml_c, st_ml_n, st_ml_m, st_gla, n_lead,
              norm_mix_even, w_in_even, ssd_conv_w, ssd_conv_b, ssd_dt_bias, ssd_a_log, ssd_d, ssd_norm,
              mlstm_conv_w, mlstm_conv_b, mlstm_if_bias, mlstm_norm, w_out_even,
              norm_ffn_even, ffn_w_gate, ffn_w_up, ffn_w_down,
              norm_mix_odd, w_in_odd, gla_w_gate_up, gla_b_gate, gla_norm, w_out_odd,
              norm_ffn_odd, moe_router, moe_w1, moe_w3, moe_w2, final_norm):
    even_new, odd_new = [], []
    for layer in range(DEPTH):
        i = layer // 2
        if layer % 2 == 0:
            mix, st = even_mixer(rmsnorm(x, norm_mix_even[i]), st_ssd_conv[i], st_ssd[i], st_ml_conv[i],
                                 st_ml_c[i], st_ml_n[i], st_ml_m[i], n_lead,
                                 w_in_even[i], ssd_conv_w[i], ssd_conv_b[i], ssd_dt_bias[i], ssd_a_log[i], ssd_d[i],
                                 ssd_norm[i], mlstm_conv_w[i], mlstm_conv_b[i], mlstm_if_bias[i], mlstm_norm[i],
                                 w_out_even[i])
            even_new.append(st)
            x = x + mix
            x = x + swiglu(rmsnorm(x, norm_ffn_even[i]), ffn_w_gate[i], ffn_w_up[i], ffn_w_down[i])
        else:
            mix, s = odd_mixer(rmsnorm(x, norm_mix_odd[i]), st_gla[i], n_lead,
                               w_in_odd[i], gla_w_gate_up[i], gla_b_gate[i], gla_norm[i], w_out_odd[i])
            odd_new.append(s)
            x = x + mix
            x = x + moe_swiglu(rmsnorm(x, norm_ffn_odd[i]), moe_router[i], moe_w1[i], moe_w3[i], moe_w2[i])
    even_stacked = [jnp.stack(s) for s in zip(*even_new)]
    return rmsnorm(x, final_norm), even_stacked, jnp.stack(odd_new)


def setup_inputs(seed: int = 0) -> dict:
    key = jax.random.key(seed)
    ks = iter(jax.random.split(key, 64))

    def nrm(shape, scale=1.0):
        return scale * jax.random.normal(next(ks), shape, jnp.float32)

    def gain(shape):
        return 1.0 + 0.05 * jax.random.normal(next(ks), shape, jnp.float32)

    def unif(shape, lo, hi):
        return jax.random.uniform(next(ks), shape, jnp.float32, minval=lo, maxval=hi)

    dt0 = jnp.exp(unif((N_EVEN, SSD_HEADS), math.log(1e-3), math.log(1e-1)))
    if_bias = jnp.concatenate([nrm((N_EVEN, MLSTM_HEADS), 0.1), unif((N_EVEN, MLSTM_HEADS), 3.0, 6.0)], axis=-1)
    return {
        "x_prompt": nrm((BATCH, SEQ, D_MODEL)),
        "x_sample": nrm((DEC_BATCH, DEC_SEQ, D_MODEL)),
        "state_ssd_conv": nrm((N_EVEN, DEC_BATCH, CONV_W - 1, SSD_CONV_DIM)),
        "state_ssd": nrm((N_EVEN, DEC_BATCH, SSD_HEADS, SSD_HEAD_DIM, SSD_STATE), 0.1),
        "state_mlstm_conv": nrm((N_EVEN, DEC_BATCH, CONV_W - 1, 2 * MLSTM_QK)),
        "state_mlstm_c": nrm((N_EVEN, DEC_BATCH, MLSTM_HEADS, MLSTM_DK, MLSTM_DV), 0.1),
        "state_mlstm_n": nrm((N_EVEN, DEC_BATCH, MLSTM_HEADS, MLSTM_DK), 0.1),
        "state_mlstm_m": unif((N_EVEN, DEC_BATCH, MLSTM_HEADS), 0.0, 3.0),
        "state_gla": nrm((N_ODD, DEC_BATCH, GLA_HEADS, GLA_DK, GLA_DV), 0.1),
        "meta_tokens": nrm((N_META, D_MODEL)),
        "norm_mix_even": gain((N_EVEN, D_MODEL)),
        "w_in_even": nrm((N_EVEN, D_MODEL, EVEN_IN), D_MODEL ** -0.5),
        "ssd_conv_w": nrm((N_EVEN, CONV_W, SSD_CONV_DIM), CONV_W ** -0.5),
        "ssd_conv_b": nrm((N_EVEN, SSD_CONV_DIM), 0.02),
        "ssd_dt_bias": dt0 + jnp.log(-jnp.expm1(-dt0)),
        "ssd_a_log": jnp.log(unif((N_EVEN, SSD_HEADS), 1.0, 16.0)),
        "ssd_d": gain((N_EVEN, SSD_HEADS)),
        "ssd_norm": gain((N_EVEN, SSD_D_INNER)),
        "mlstm_conv_w": nrm((N_EVEN, CONV_W, 2 * MLSTM_QK), CONV_W ** -0.5),
        "mlstm_conv_b": nrm((N_EVEN, 2 * MLSTM_QK), 0.02),
        "mlstm_if_bias": if_bias,
        "mlstm_norm": gain((N_EVEN, MLSTM_V)),
        "w_out_even": nrm((N_EVEN, EVEN_MIX, D_MODEL), EVEN_MIX ** -0.5),
        "norm_ffn_even": gain((N_EVEN, D_MODEL)),
        "ffn_w_gate": nrm((N_EVEN, D_MODEL, D_FF), D_MODEL ** -0.5),
        "ffn_w_up": nrm((N_EVEN, D_MODEL, D_FF), D_MODEL ** -0.5),
        "ffn_w_down": nrm((N_EVEN, D_FF, D_MODEL), D_FF ** -0.5),
        "norm_mix_odd": gain((N_ODD, D_MODEL)),
        "w_in_odd": nrm((N_ODD, D_MODEL, ODD_IN), D_MODEL ** -0.5),
        "gla_w_gate_up": nrm((N_ODD, GLA_RANK, GLA_HEADS * GLA_DK), GLA_RANK ** -0.5),
        "gla_b_gate": nrm((N_ODD, GLA_HEADS * GLA_DK), 0.1),
        "gla_norm": gain((N_ODD, ODD_MIX)),
        "w_out_odd": nrm((N_ODD, ODD_MIX, D_MODEL), ODD_MIX ** -0.5),
        "norm_ffn_odd": gain((N_ODD, D_MODEL)),
        "moe_router": nrm((N_ODD, D_MODEL, N_EXPERTS), D_MODEL ** -0.5),
        "moe_w1": nrm((N_ODD, N_EXPERTS, D_MODEL, D_FF_E), D_MODEL ** -0.5),
        "moe_w3": nrm((N_ODD, N_EXPERTS, D_MODEL, D_FF_E), D_MODEL ** -0.5),
        "moe_w2": nrm((N_ODD, N_EXPERTS, D_FF_E, D_MODEL), D_FF_E ** -0.5),
        "final_norm": gain((D_MODEL,)),
    }


def reference(x_prompt, x_sample, state_ssd_conv, state_ssd, state_mlstm_conv, state_mlstm_c, state_mlstm_n,
              state_mlstm_m, state_gla, meta_tokens,
              norm_mix_even, w_in_even, ssd_conv_w, ssd_conv_b, ssd_dt_bias, ssd_a_log, ssd_d, ssd_norm,
              mlstm_conv_w, mlstm_conv_b, mlstm_if_bias, mlstm_norm, w_out_even,
              norm_ffn_even, ffn_w_gate, ffn_w_up, ffn_w_down,
              norm_mix_odd, w_in_odd, gla_w_gate_up, gla_b_gate, gla_norm, w_out_odd,
              norm_ffn_odd, moe_router, moe_w1, moe_w3, moe_w2, final_norm):
    weights = (norm_mix_even, w_in_even, ssd_conv_w, ssd_conv_b, ssd_dt_bias, ssd_a_log, ssd_d, ssd_norm,
               mlstm_conv_w, mlstm_conv_b, mlstm_if_bias, mlstm_norm, w_out_even,
               norm_ffn_even, ffn_w_gate, ffn_w_up, ffn_w_down,
               norm_mix_odd, w_in_odd, gla_w_gate_up, gla_b_gate, gla_norm, w_out_odd,
               norm_ffn_odd, moe_router, moe_w1, moe_w3, moe_w2, final_norm)
    bp = x_prompt.shape[0]
    pdt = x_prompt.dtype
    meta = jnp.broadcast_to(meta_tokens.astype(pdt)[None], (bp, N_META, D_MODEL))
    x0 = jnp.concatenate([meta, x_prompt], axis=1)
    p_out, p_even, p_gla = run_trunk(
        x0,
        jnp.zeros((N_EVEN, bp, CONV_W - 1, SSD_CONV_DIM), pdt),
        jnp.zeros((N_EVEN, bp, SSD_HEADS, SSD_HEAD_DIM, SSD_STATE), pdt),
        jnp.zeros((N_EVEN, bp, CONV_W - 1, 2 * MLSTM_QK), pdt),
        jnp.zeros((N_EVEN, bp, MLSTM_HEADS, MLSTM_DK, MLSTM_DV), pdt),
        jnp.zeros((N_EVEN, bp, MLSTM_HEADS, MLSTM_DK), pdt),
        jnp.zeros((N_EVEN, bp, MLSTM_HEADS), pdt),
        jnp.zeros((N_ODD, bp, GLA_HEADS, GLA_DK, GLA_DV), pdt),
        N_META, *weights)
    y_prompt = p_out[:, N_META:]
    p_ssd_conv, p_ssd, p_ml_conv, p_ml_c, p_ml_n, p_ml_m = p_even
    s_out, s_even, s_gla = run_trunk(
        x_sample, state_ssd_conv, state_ssd, state_mlstm_conv, state_mlstm_c, state_mlstm_n, state_mlstm_m,
        state_gla, 0, *weights)
    y_sample = s_out
    s_ssd_conv, s_ssd, s_ml_conv, s_ml_c, s_ml_n, s_ml_m = s_even
    return (y_prompt, y_sample, p_ssd_conv, p_ssd, p_ml_conv, p_ml_c, p_ml_n, p_ml_m, p_gla,
            s_ssd_conv, s_ssd, s_ml_conv, s_ml_c, s_ml_n, s_ml_m, s_gla)
```

```python
import functools
import math

import jax
import jax.numpy as jnp
from jax import lax
from jax.experimental import pallas as pl
from jax.experimental.pallas import tpu as pltpu

F32 = jnp.float32
BF16 = jnp.bfloat16

EPS = 1e-6
CONV_W = 4
SSD_GROUPS = 8
GLA_TAU = 16.0
N_EXPERTS_PAD = 128
CHUNK_PROMPT = 64

V7X_VMEM_BYTES = 64 * 1024 * 1024
VMEM_LIMIT = V7X_VMEM_BYTES - 8 * 1024 * 1024
VMEM_BUDGET = VMEM_LIMIT - 4 * 1024 * 1024
LANE = 128
SUBLANE = 8
TK_MAX = 4096


def _cparams(n_axes):
    return pltpu.CompilerParams(dimension_semantics=("arbitrary",) * n_axes, vmem_limit_bytes=VMEM_LIMIT)


def _pick(n, candidates):
    for c in candidates:
        if c <= n and n % c == 0:
            return c
    return n


def _row_tile(n, cap):
    if n <= cap:
        return n
    best = SUBLANE
    for c in range(SUBLANE, cap + 1, SUBLANE):
        if n % c == 0:
            best = c
    return best


def _dot(a, b, dims=((1,), (0,))):
    return lax.dot_general(a.astype(BF16), b.astype(BF16), (dims, ((), ())), preferred_element_type=F32)


def _split3(x):
    hi = x.astype(BF16)
    r1 = x - hi.astype(F32)
    mid = r1.astype(BF16)
    lo = (r1 - mid.astype(F32)).astype(BF16)
    return hi, mid, lo


def _dot_exact_rhs(a01, b, dims=((1,), (0,))):
    a = a01.astype(BF16)
    out = None
    for part in _split3(b):
        t = lax.dot_general(a, part, (dims, ((), ())), preferred_element_type=F32)
        out = t if out is None else out + t
    return out


def _dot_exact_lhs(a, b01, dims=((1,), (0,))):
    b = b01.astype(BF16)
    out = None
    for part in _split3(a):
        t = lax.dot_general(part, b, (dims, ((), ())), preferred_element_type=F32)
        out = t if out is None else out + t
    return out


def _iota2(shape, axis):
    return lax.broadcasted_iota(jnp.int32, shape, axis)


def _tril(q):
    return _iota2((q, q), 0) >= _iota2((q, q), 1)


def _transpose_small(x, q):
    eye = (_iota2((q, q), 0) == _iota2((q, q), 1)).astype(F32)
    return _dot_exact_lhs(x, eye, dims=((0,), (0,)))


def _silu(x):
    return x * jax.nn.sigmoid(x)


def _rms(x, g):
    return x * lax.rsqrt(jnp.mean(x * x, axis=-1, keepdims=True) + EPS) * g


def _rmsnorm_body(x_ref, g_ref, o_ref):
    o_ref[...] = _rms(x_ref[...], g_ref[...]).astype(o_ref.dtype)


def _rmsnorm(x, g, out_dtype=BF16):
    r, d = x.shape
    tr = _row_tile(r, 256)
    return pl.pallas_call(
        _rmsnorm_body,
        out_shape=jax.ShapeDtypeStruct((r, d), out_dtype),
        grid=(r // tr,),
        in_specs=[pl.BlockSpec((tr, d), lambda i: (i, 0)), pl.BlockSpec((1, d), lambda i: (0, 0))],
        out_specs=pl.BlockSpec((tr, d), lambda i: (i, 0)),
        compiler_params=_cparams(1),
        name="rmsnorm",
    )(x, g.reshape(1, d))


def _gmm_body(te_ref, tf_ref, tv_ref, tr_ref, x_ref, w_ref, *rest, kc, has_res, has_scale):
    del te_ref, tr_ref
    rest = list(rest)
    res_ref = rest.pop(0) if has_res else None
    scale_ref = rest.pop(0) if has_scale else None
    o_ref, wbf_ref = rest[0], rest[1]
    acc_ref = rest[2] if kc > 1 else None
    t = pl.program_id(1)
    c = pl.program_id(2)

    @pl.when(tf_ref[t] == 1)
    def _cast():
        wbf_ref[c] = w_ref[...].astype(BF16)

    def finish(v):
        if has_scale:
            v = v * scale_ref[...]
        if has_res:
            v = v + res_ref[...]
        o_ref[...] = v.astype(o_ref.dtype)

    @pl.when(tv_ref[t] == 1)
    def _compute():
        part = jnp.dot(x_ref[...], wbf_ref[c], preferred_element_type=F32)
        if kc == 1:
            finish(part)
        else:
            @pl.when(c == 0)
            def _():
                acc_ref[...] = part

            @pl.when(c > 0)
            def _():
                acc_ref[...] += part

            @pl.when(c == kc - 1)
            def _():
                finish(acc_ref[...])

    @pl.when(tv_ref[t] == 0)
    def _pad():
        o_ref[...] = jnp.zeros_like(o_ref)


def _dense_tiles(n_tiles):
    t = jnp.arange(n_tiles, dtype=jnp.int32)
    return (jnp.zeros((n_tiles,), jnp.int32), (t == 0).astype(jnp.int32), jnp.ones((n_tiles,), jnp.int32), t)


def _tk_options(k):
    if k <= TK_MAX:
        return [k]
    divs = [c for c in range(LANE, k, LANE) if k % c == 0]
    good = sorted((c for c in divs if TK_MAX // 2 <= c <= TK_MAX), reverse=True)
    over = sorted(c for c in divs if TK_MAX < c <= 2 * TK_MAX)
    return good + over or sorted(divs, reverse=True)


def _tm_options(rows):
    opts = [c for c in range(2 * SUBLANE, min(rows, 1024) + 1, 2 * SUBLANE) if rows % c == 0]
    return sorted(opts, reverse=True) or [rows]


def _plan_mm(k, n_cols, col0, tm_options, has_res):
    for tn in (512, 256, 128):
        if n_cols % tn or col0 % tn:
            continue
        for tm in tm_options:
            for tk in _tk_options(k):
                kc = k // tk
                need = (2 * tk * tn * 4 + k * tn * 2 + 2 * tm * tk * 2 + 3 * tm * tn * 4
                        + (2 * tm * tn * 4 if has_res else 0) + (tm * tn * 4 if kc > 1 else 0))
                if need <= VMEM_BUDGET:
                    return tm, tn, tk
    raise ValueError(f"no matmul tiling fits VMEM for k={k} n={n_cols}")


def _plan_swiglu(k, f, tm_options):
    for tn in (512, 256, 128):
        if f % tn:
            continue
        for tm in tm_options:
            need = 2 * (2 * k * tn * 4) + 2 * k * tn * 2 + 2 * tm * k * 2 + 2 * tm * tn * 2 + 3 * tm * tn * 4
            if need <= VMEM_BUDGET:
                return tm, tn
    raise ValueError(f"no swiglu tiling fits VMEM for k={k} f={f}")


def _gmm(x, w, tiles, *, tm, tn, tk, n_cols=None, col0=0, res=None, scale=None, out_dtype=F32, name="gmm"):
    te, tf, tv, tr = tiles
    n_tiles = te.shape[0]
    rows, k = x.shape
    n_total = w.shape[2]
    n_cols = n_total if n_cols is None else n_cols
    assert col0 % tn == 0 and n_cols % tn == 0 and rows % tm == 0 and k % tk == 0
    kc = k // tk
    j0 = col0 // tn

    def x_map(j, t, c, te, tf, tv, tr):
        return (tr[t], jnp.where(tv[t] == 1, c, kc - 1))

    def w_map(j, t, c, te, tf, tv, tr):
        return (te[t], jnp.where(tf[t] == 1, c, kc - 1), j0 + j)

    def o_map(j, t, c, te, tf, tv, tr):
        return (tr[t], j)

    in_specs = [pl.BlockSpec((tm, tk), x_map), pl.BlockSpec((None, tk, tn), w_map)]
    args = [x, w]
    if res is not None:
        in_specs.append(pl.BlockSpec((tm, tn), o_map))
        args.append(res)
    if scale is not None:
        in_specs.append(pl.BlockSpec((tm, 1), lambda j, t, c, te, tf, tv, tr: (tr[t], 0)))
        args.append(scale)
    scratch = [pltpu.VMEM((kc, tk, tn), BF16)]
    if kc > 1:
        scratch.append(pltpu.VMEM((tm, tn), F32))
    return pl.pallas_call(
        functools.partial(_gmm_body, kc=kc, has_res=res is not None, has_scale=scale is not None),
        out_shape=jax.ShapeDtypeStruct((rows, n_cols), out_dtype),
        grid_spec=pltpu.PrefetchScalarGridSpec(
            num_scalar_prefetch=4,
            grid=(n_cols // tn, n_tiles, kc),
            in_specs=in_specs,
            out_specs=pl.BlockSpec((tm, tn), o_map),
            scratch_shapes=scratch,
        ),
        compiler_params=_cparams(3),
        name=name,
    )(te, tf, tv, tr, *args)


def _gmm_swiglu_body(te_ref, tf_ref, tv_ref, tr_ref, x_ref, wg_ref, wu_ref, o_ref, wgb_ref, wub_ref):
    del te_ref, tr_ref
    t = pl.program_id(1)

    @pl.when(tf_ref[t] == 1)
    def _cast():
        wgb_ref[...] = wg_ref[...].astype(BF16)
        wub_ref[...] = wu_ref[...].astype(BF16)

    @pl.when(tv_ref[t] == 1)
    def _compute():
        x = x_ref[...]
        g = jnp.dot(x, wgb_ref[...], preferred_element_type=F32)
        u = jnp.dot(x, wub_ref[...], preferred_element_type=F32)
        o_ref[...] = (_silu(g) * u).astype(o_ref.dtype)

    @pl.when(tv_ref[t] == 0)
    def _pad():
        o_ref[...] = jnp.zeros_like(o_ref)


def _gmm_swiglu(x, wg, wu, tiles, *, tm, tn, name="gmm_swiglu"):
    te, tf, tv, tr = tiles
    n_tiles = te.shape[0]
    rows, k = x.shape
    f = wg.shape[2]
    assert rows % tm == 0 and f % tn == 0

    def x_map(j, t, te, tf, tv, tr):
        return (tr[t], 0)

    def w_map(j, t, te, tf, tv, tr):
        return (te[t], 0, j)

    def o_map(j, t, te, tf, tv, tr):
        return (tr[t], j)

    return pl.pallas_call(
        _gmm_swiglu_body,
        out_shape=jax.ShapeDtypeStruct((rows, f), BF16),
        grid_spec=pltpu.PrefetchScalarGridSpec(
            num_scalar_prefetch=4,
            grid=(f // tn, n_tiles),
            in_specs=[pl.BlockSpec((tm, k), x_map), pl.BlockSpec((None, k, tn), w_map),
                      pl.BlockSpec((None, k, tn), w_map)],
            out_specs=pl.BlockSpec((tm, tn), o_map),
            scratch_shapes=[pltpu.VMEM((k, tn), BF16), pltpu.VMEM((k, tn), BF16)],
        ),
        compiler_params=_cparams(2),
        name=name,
    )(te, tf, tv, tr, x, wg, wu)


def _dense_mm(x, w, *, n_cols=None, col0=0, res=None, name):
    rows, k = x.shape
    n = w.shape[2] if n_cols is None else n_cols
    tm, tn, tk = _plan_mm(k, n, col0, _tm_options(rows), res is not None)
    return _gmm(x, w, _dense_tiles(rows // tm), tm=tm, tn=tn, tk=tk, n_cols=n_cols, col0=col0, res=res, name=name)


def _dense_swiglu(x, wg, wu, name):
    rows, k = x.shape
    tm, tn = _plan_swiglu(k, wg.shape[2], _tm_options(rows))
    return _gmm_swiglu(x, wg, wu, _dense_tiles(rows // tm), tm=tm, tn=tn, name=name)


def _conv_body(prev_ref, u_ref, w_ref, b_ref, o_ref, full_ref, *, seq):
    full_ref[0:SUBLANE, :] = prev_ref[...]
    full_ref[SUBLANE:SUBLANE + seq, :] = u_ref[...]
    acc = b_ref[...]
    for k in range(CONV_W):
        start = SUBLANE - (CONV_W - 1) + k
        acc = acc + full_ref[start:start + seq, :] * w_ref[k:k + 1, :]
    o_ref[...] = _silu(acc)


def _conv_silu(u_arr, col0, c, row0, nb, seq, prev, w, b, name):
    ct = _pick(c, (128,)) if seq > 64 else _pick(c, (2048, 1024, 512, 256, 128))
    assert col0 % ct == 0 and row0 % seq == 0
    rb0, cb0 = row0 // seq, col0 // ct
    return pl.pallas_call(
        functools.partial(_conv_body, seq=seq),
        out_shape=jax.ShapeDtypeStruct((nb * seq, c), F32),
        grid=(nb, c // ct),
        in_specs=[pl.BlockSpec((None, SUBLANE, ct), lambda b_, j: (b_, 0, j)),
                  pl.BlockSpec((seq, ct), lambda b_, j: (rb0 + b_, cb0 + j)),
                  pl.BlockSpec((CONV_W, ct), lambda b_, j: (0, j)),
                  pl.BlockSpec((1, ct), lambda b_, j: (0, j))],
        out_specs=pl.BlockSpec((seq, ct), lambda b_, j: (b_, j)),
        scratch_shapes=[pltpu.VMEM((SUBLANE + seq, ct), F32)],
        compiler_params=_cparams(2),
        name=name,
    )(prev, u_arr, w, b.reshape(1, c))


def _ssd_body(*refs, q, hpg, p, zero_init):
    if zero_init:
        x_ref, b_ref, c_ref, z_ref, dt_ref, par_ref, gn_ref, y_ref, h_ref, ys_ref = refs
        h0_ref = None
    else:
        x_ref, b_ref, c_ref, z_ref, dt_ref, par_ref, gn_ref, h0_ref, y_ref, h_ref, ys_ref = refs
    ci = pl.program_id(2)

    @pl.when(ci == 0)
    def _init():
        if zero_init:
            h_ref[...] = jnp.zeros_like(h_ref)
        else:
            h_ref[...] = h0_ref[...]

    x = x_ref[...]
    bm = b_ref[...]
    cm = c_ref[...]
    par = par_ref[...]
    dt = jax.nn.softplus(dt_ref[...] + par[0:1, :])
    a = -jnp.exp(par[1:2, :])
    dta = dt * a
    tril = _tril(q)
    cum = _dot_exact_rhs(tril.astype(F32), dta)
    cum_t = _transpose_small(cum, q)
    dt_t = _transpose_small(dt, q)
    hstate = h_ref[...]
    cb = _dot(cm, bm, dims=((1,), (1,)))
    y_inter = _dot(cm, hstate, dims=((1,), (1,)))
    cum_last = cum[q - 1:q, :]
    w_end = jnp.exp(cum_last - cum) * dt
    ecum = jnp.exp(cum)
    expand = (_iota2((hpg, hpg * p), 1) // p == _iota2((hpg, hpg * p), 0)).astype(F32)
    ecum_w = _dot_exact_lhs(ecum, expand)
    wend_w = _dot_exact_lhs(w_end, expand)
    d_w = _dot_exact_lhs(par[2:3, :], expand)
    for h in range(hpg):
        seg = cum[:, h:h + 1] - cum_t[h:h + 1, :]
        lmat = jnp.exp(jnp.where(tril, seg, -jnp.inf))
        scores = cb * lmat * dt_t[h:h + 1, :]
        ys_ref[:, h * p:(h + 1) * p] = _dot(scores, x[:, h * p:(h + 1) * p])
    y = ys_ref[...] + y_inter * ecum_w + d_w * x
    y = y * _silu(z_ref[...])
    y_ref[...] = _rms(y, gn_ref[...]).astype(y_ref.dtype)
    expand_t = (_iota2((hpg * p, hpg), 0) // p == _iota2((hpg * p, hpg), 1)).astype(F32)
    last_col = jnp.broadcast_to(cum_t[:, q - 1:q], (hpg, LANE))
    decay = jnp.exp(_dot_exact_rhs(expand_t, last_col))
    n = hstate.shape[1]
    if n != LANE:
        decay = jnp.broadcast_to(decay[:, 0:1], hstate.shape)
    h_ref[...] = decay * hstate + _dot(x * wend_w, bm, dims=((0,), (0,)))


def _ssd(xbc_c, xw_a, z_col0, dt_g, par_g, gnorm, h0, row0, nb, seq, q, hpg, p, n, name):
    g_cnt = SSD_GROUPS
    nc = seq // q
    gw = hpg * p
    d_inner = g_cnt * gw
    assert row0 % q == 0 and z_col0 % gw == 0 and d_inner % n == 0
    rb0 = row0 // q
    zb0 = z_col0 // gw
    bb0 = d_inner // n
    zero_init = h0 is None

    in_specs = [
        pl.BlockSpec((q, gw), lambda b, g, c: (b * nc + c, g)),
        pl.BlockSpec((q, n), lambda b, g, c: (b * nc + c, bb0 + g)),
        pl.BlockSpec((q, n), lambda b, g, c: (b * nc + c, bb0 + g_cnt + g)),
        pl.BlockSpec((q, gw), lambda b, g, c: (rb0 + b * nc + c, zb0 + g)),
        pl.BlockSpec((None, q, hpg), lambda b, g, c: (g, rb0 + b * nc + c, 0)),
        pl.BlockSpec((None, 3, hpg), lambda b, g, c: (g, 0, 0)),
        pl.BlockSpec((1, gw), lambda b, g, c: (0, g)),
    ]
    args = [xbc_c, xbc_c, xbc_c, xw_a, dt_g, par_g, gnorm]
    if not zero_init:
        in_specs.append(pl.BlockSpec((None, None, gw, n), lambda b, g, c: (b, g, 0, 0)))
        args.append(h0)
    return pl.pallas_call(
        functools.partial(_ssd_body, q=q, hpg=hpg, p=p, zero_init=zero_init),
        out_shape=(jax.ShapeDtypeStruct((nb * seq, d_inner), BF16),
                   jax.ShapeDtypeStruct((nb, g_cnt, gw, n), F32)),
        grid=(nb, g_cnt, nc),
        in_specs=in_specs,
        out_specs=(pl.BlockSpec((q, gw), lambda b, g, c: (b * nc + c, g)),
                   pl.BlockSpec((None, None, gw, n), lambda b, g, c: (b, g, 0, 0))),
        scratch_shapes=[pltpu.VMEM((q, gw), F32)],
        compiler_params=_cparams(3),
        name=name,
    )(*args)


def _mlstm_body(*refs, q, dk, zero_init):
    if zero_init:
        (q_ref, k_ref, v_ref, o_ref, gt_ref, gb_ref, gn_ref, y_ref, c_ref, n_ref, m_ref) = refs
        c0_ref = n0_ref = m0_ref = None
    else:
        (q_ref, k_ref, v_ref, o_ref, gt_ref, gb_ref, gn_ref, c0_ref, n0_ref, m0_ref,
         y_ref, c_ref, n_ref, m_ref) = refs
    ci = pl.program_id(2)

    @pl.when(ci == 0)
    def _init():
        if zero_init:
            c_ref[...] = jnp.zeros_like(c_ref)
            n_ref[...] = jnp.zeros_like(n_ref)
            m_ref[...] = jnp.zeros_like(m_ref)
        else:
            c_ref[...] = c0_ref[...]
            n_ref[...] = n0_ref[...]
            m_ref[...] = m0_ref[...]

    qh = q_ref[...] * (dk ** -0.5)
    kh = k_ref[...]
    vh = v_ref[...]
    gates = gt_ref[...] + gb_ref[...]
    log_i = gates[:, 0:1]
    log_f = jax.nn.log_sigmoid(gates[:, 1:2])
    tril = _tril(q)
    cumf = _dot_exact_rhs(tril.astype(F32), jnp.broadcast_to(log_f, (q, LANE)))[:, 0:1]
    both = jnp.where(_iota2((q, LANE), 1) == 0, cumf, log_i)
    rows = _transpose_small(both, q)
    cumf_r, logi_r = rows[0:1, :], rows[1:2, :]
    m_prev = m_ref[...]
    cmat = c_ref[...]
    nvec = n_ref[...]
    dmat = jnp.where(tril, cumf - cumf_r + logi_r, -jnp.inf)
    from_state = cumf + m_prev
    m_row = jnp.maximum(jnp.max(dmat, axis=1, keepdims=True), from_state)
    w = jnp.exp(dmat - m_row)
    s_state = jnp.exp(from_state - m_row)
    qk = _dot(qh, kh, dims=((1,), (1,))) * w
    num = _dot(qk, vh) + _dot(qh, cmat) * s_state
    den = jnp.sum(qk, axis=1, keepdims=True) + jnp.sum(qh * nvec, axis=1, keepdims=True) * s_state
    den = jnp.maximum(jnp.abs(den), jnp.exp(-m_row))
    hout = num / den
    m_new = m_row[q - 1:q, :]
    cumf_last = cumf[q - 1:q, :]
    w_end = jnp.exp(cumf_last - cumf + log_i - m_new)
    s_end = jnp.exp(cumf_last + m_prev - m_new)
    kw = kh * w_end
    c_ref[...] = s_end * cmat + _dot(kw, vh, dims=((0,), (0,)))
    n_ref[...] = s_end * nvec + jnp.sum(kw, axis=0, keepdims=True)
    m_ref[...] = m_new
    y_ref[...] = (_rms(hout, gn_ref[...]) * jax.nn.sigmoid(o_ref[...])).astype(y_ref.dtype)


def _mlstm(qk_c, xw_b, v_col0, o_col0, gt_h, gb_h, gnorm, state0, row0, nb, seq, q, heads, dk, dv, name):
    nc = seq // q
    assert row0 % q == 0 and v_col0 % dv == 0 and o_col0 % dv == 0
    rb0 = row0 // q
    vb0, ob0 = v_col0 // dv, o_col0 // dv
    zero_init = state0 is None
    in_specs = [
        pl.BlockSpec((q, dk), lambda b, h, c: (b * nc + c, h)),
        pl.BlockSpec((q, dk), lambda b, h, c: (b * nc + c, heads + h)),
        pl.BlockSpec((q, dv), lambda b, h, c: (rb0 + b * nc + c, vb0 + h)),
        pl.BlockSpec((q, dv), lambda b, h, c: (rb0 + b * nc + c, ob0 + h)),
        pl.BlockSpec((None, q, 2), lambda b, h, c: (h, rb0 + b * nc + c, 0)),
        pl.BlockSpec((None, 1, 2), lambda b, h, c: (h, 0, 0)),
        pl.BlockSpec((1, dv), lambda b, h, c: (0, h)),
    ]
    args = [qk_c, qk_c, xw_b, xw_b, gt_h, gb_h, gnorm]
    st_specs = (pl.BlockSpec((None, None, dk, dv), lambda b, h, c: (b, h, 0, 0)),
                pl.BlockSpec((None, None, 1, dk), lambda b, h, c: (b, h, 0, 0)),
                pl.BlockSpec((None, None, 1, 1), lambda b, h, c: (b, h, 0, 0)))
    if not zero_init:
        in_specs.extend(st_specs)
        args.extend(state0)
    return pl.pallas_call(
        functools.partial(_mlstm_body, q=q, dk=dk, zero_init=zero_init),
        out_shape=(jax.ShapeDtypeStruct((nb * seq, heads * dv), BF16),
                   jax.ShapeDtypeStruct((nb, heads, dk, dv), F32),
                   jax.ShapeDtypeStruct((nb, heads, 1, dk), F32),
                   jax.ShapeDtypeStruct((nb, heads, 1, 1), F32)),
        grid=(nb, heads, nc),
        in_specs=in_specs,
        out_specs=(pl.BlockSpec((q, dv), lambda b, h, c: (b * nc + c, h)),) + st_specs,
        compiler_params=_cparams(3),
        name=name,
    )(*args)


def _gla_body(*refs, q, dk, zero_init):
    if zero_init:
        q_ref, k_ref, v_ref, r_ref, ga_ref, wgu_ref, bg_ref, gn_ref, y_ref, s_ref = refs
        s0_ref = None
    else:
        q_ref, k_ref, v_ref, r_ref, ga_ref, wgu_ref, bg_ref, gn_ref, s0_ref, y_ref, s_ref = refs
    ci = pl.program_id(2)

    @pl.when(ci == 0)
    def _init():
        if zero_init:
            s_ref[...] = jnp.zeros_like(s_ref)
        else:
            s_ref[...] = s0_ref[...]

    kh = k_ref[...]
    vh = v_ref[...]
    gate_pre = _dot(ga_ref[...], wgu_ref[...]) + bg_ref[...]
    log_a = jax.nn.log_sigmoid(gate_pre) / GLA_TAU
    tril = _tril(q)
    bcum = _dot_exact_rhs(tril.astype(F32), log_a)
    qt = q_ref[...] * (dk ** -0.5) * jnp.exp(bcum)
    kt = kh * jnp.exp(-bcum)
    att = jnp.where(tril, _dot(qt, kt, dims=((1,), (1,))), 0.0)
    smat = s_ref[...]
    o = _dot(att, vh) + _dot(qt, smat)
    b_last = bcum[q - 1:q, :]
    kw = kh * jnp.exp(b_last - bcum)
    tot = _dot_exact_lhs(log_a, jnp.ones((q, LANE), F32), dims=((0,), (0,)))
    decay = jnp.exp(tot[:, 0:1])
    s_ref[...] = decay * smat + _dot(kw, vh, dims=((0,), (0,)))
    y_ref[...] = (_rms(o, gn_ref[...]) * _silu(r_ref[...])).astype(y_ref.dtype)


def _gla(xw, q_col0, k_col0, v_col0, r_col0, small, wgu, bg, gnorm, s0, row0, nb, seq, q, heads, dk, dv, name):
    nc = seq // q
    assert row0 % q == 0
    rb0 = row0 // q
    qb0, kb0, vb0, rcb0 = q_col0 // dk, k_col0 // dk, v_col0 // dv, r_col0 // dv
    zero_init = s0 is None
    in_specs = [
        pl.BlockSpec((q, dk), lambda b, h, c: (rb0 + b * nc + c, qb0 + h)),
        pl.BlockSpec((q, dk), lambda b, h, c: (rb0 + b * nc + c, kb0 + h)),
        pl.BlockSpec((q, dv), lambda b, h, c: (rb0 + b * nc + c, vb0 + h)),
        pl.BlockSpec((q, dv), lambda b, h, c: (rb0 + b * nc + c, rcb0 + h)),
        pl.BlockSpec((q, LANE), lambda b, h, c: (rb0 + b * nc + c, 0)),
        pl.BlockSpec((LANE, dk), lambda b, h, c: (0, h)),
        pl.BlockSpec((1, dk), lambda b, h, c: (0, h)),
        pl.BlockSpec((1, dv), lambda b, h, c: (0, h)),
    ]
    args = [xw, xw, xw, xw, small, wgu, bg, gnorm]
    st_spec = pl.BlockSpec((None, None, dk, dv), lambda b, h, c: (b, h, 0, 0))
    if not zero_init:
        in_specs.append(st_spec)
        args.append(s0)
    return pl.pallas_call(
        functools.partial(_gla_body, q=q, dk=dk, zero_init=zero_init),
        out_shape=(jax.ShapeDtypeStruct((nb * seq, heads * dv), BF16),
                   jax.ShapeDtypeStruct((nb, heads, dk, dv), F32)),
        grid=(nb, heads, nc),
        in_specs=in_specs,
        out_specs=(pl.BlockSpec((q, dv), lambda b, h, c: (b * nc + c, h)), st_spec),
        compiler_params=_cparams(3),
        name=name,
    )(*args)


def _router_body(x_ref, g_ref, wr_ref, idx_ref, gate_ref, *, n_experts):
    xn = _rms(x_ref[...], g_ref[...])
    wr = wr_ref[...]
    xh = xn.astype(BF16)
    xl = (xn - xh.astype(F32)).astype(BF16)
    wh = wr.astype(BF16)
    wl = (wr - wh.astype(F32)).astype(BF16)
    logits = (jnp.dot(xh, wh, preferred_element_type=F32) + jnp.dot(xh, wl, preferred_element_type=F32)
              + jnp.dot(xl, wh, preferred_element_type=F32))
    lane = _iota2(logits.shape, 1)
    neg = -jnp.inf
    l1 = jnp.where(lane < n_experts, logits, neg)
    m1 = jnp.max(l1, axis=1, keepdims=True)
    i1 = jnp.min(jnp.where(l1 == m1, lane, N_EXPERTS_PAD), axis=1, keepdims=True)
    l2 = jnp.where(lane == i1, neg, l1)
    m2 = jnp.max(l2, axis=1, keepdims=True)
    i2 = jnp.min(jnp.where(l2 == m2, lane, N_EXPERTS_PAD), axis=1, keepdims=True)
    e2 = jnp.exp(m2 - m1)
    denom = 1.0 + e2
    idx_ref[...] = jnp.where(lane == 0, i1, jnp.where(lane == 1, i2, 0))
    gate_ref[...] = jnp.where(lane == 0, 1.0 / denom, jnp.where(lane == 1, e2 / denom, 0.0))


def _router(x, g, w_router):
    r, d = x.shape
    n_experts = w_router.shape[1]
    wr = jnp.pad(w_router, ((0, 0), (0, N_EXPERTS_PAD - n_experts)))
    tr = _row_tile(r, 256)
    return pl.pallas_call(
        functools.partial(_router_body, n_experts=n_experts),
        out_shape=(jax.ShapeDtypeStruct((r, N_EXPERTS_PAD), jnp.int32),
                   jax.ShapeDtypeStruct((r, N_EXPERTS_PAD), F32)),
        grid=(r // tr,),
        in_specs=[pl.BlockSpec((tr, d), lambda i: (i, 0)), pl.BlockSpec((1, d), lambda i: (0, 0)),
                  pl.BlockSpec((d, N_EXPERTS_PAD), lambda i: (0, 0))],
        out_specs=(pl.BlockSpec((tr, N_EXPERTS_PAD), lambda i: (i, 0)),
                   pl.BlockSpec((tr, N_EXPERTS_PAD), lambda i: (i, 0))),
        compiler_params=_cparams(1),
        name="moe_router",
    )(x, g.reshape(1, d), wr)


def _row_copy(src_hbm, row, dst_vmem, slot, sem):
    return pltpu.make_async_copy(src_hbm.at[pl.ds(row, 1)], dst_vmem.at[pl.ds(slot, 1)], sem)


def _gather_norm_body(src_ref, x_hbm, g_ref, o_ref, buf_ref, sem, *, tg):
    base = pl.program_id(0) * tg

    def issue(r, carry):
        _row_copy(x_hbm, src_ref[base + r], buf_ref, r, sem).start()
        return carry

    lax.fori_loop(0, tg, issue, 0)

    def wait(r, carry):
        _row_copy(x_hbm, 0, buf_ref, r, sem).wait()
        return carry

    lax.fori_loop(0, tg, wait, 0)
    o_ref[...] = _rms(buf_ref[...], g_ref[...]).astype(o_ref.dtype)


def _gather_norm(x, g, src_rows, tg):
    _, d = x.shape
    p_rows = src_rows.shape[0]
    return pl.pallas_call(
        functools.partial(_gather_norm_body, tg=tg),
        out_shape=jax.ShapeDtypeStruct((p_rows, d), BF16),
        grid_spec=pltpu.PrefetchScalarGridSpec(
            num_scalar_prefetch=1,
            grid=(p_rows // tg,),
            in_specs=[pl.BlockSpec(memory_space=pl.ANY), pl.BlockSpec((1, d), lambda i, s: (0, 0))],
            out_specs=pl.BlockSpec((tg, d), lambda i, s: (i, 0)),
            scratch_shapes=[pltpu.VMEM((tg, d), F32), pltpu.SemaphoreType.DMA(())],
        ),
        compiler_params=_cparams(1),
        name="moe_gather_norm",
    )(src_rows, x, g.reshape(1, d))


def _combine_body(pos_ref, x_ref, y_hbm, g_ref, o_ref, buf_ref, sem, *, tc, top_k):
    base = pl.program_id(0) * tc

    def issue(r, carry):
        for k in range(top_k):
            _row_copy(y_hbm, pos_ref[(base + r) * top_k + k], buf_ref.at[k], r, sem).start()
        return carry

    lax.fori_loop(0, tc, issue, 0)

    def wait(r, carry):
        for k in range(top_k):
            _row_copy(y_hbm, 0, buf_ref.at[k], r, sem).wait()
        return carry

    lax.fori_loop(0, tc, wait, 0)
    y = buf_ref[0]
    for k in range(1, top_k):
        y = y + buf_ref[k]
    o_ref[...] = _rms(x_ref[...] + y, g_ref[...])


def _combine_norm(x, y_sorted, pos, g, top_k):
    r, d = x.shape
    tc = _row_tile(r, 256)
    return pl.pallas_call(
        functools.partial(_combine_body, tc=tc, top_k=top_k),
        out_shape=jax.ShapeDtypeStruct((r, d), F32),
        grid_spec=pltpu.PrefetchScalarGridSpec(
            num_scalar_prefetch=1,
            grid=(r // tc,),
            in_specs=[pl.BlockSpec((tc, d), lambda i, s: (i, 0)), pl.BlockSpec(memory_space=pl.ANY),
                      pl.BlockSpec((1, d), lambda i, s: (0, 0))],
            out_specs=pl.BlockSpec((tc, d), lambda i, s: (i, 0)),
            scratch_shapes=[pltpu.VMEM((top_k, tc, d), F32), pltpu.SemaphoreType.DMA(())],
        ),
        compiler_params=_cparams(1),
        name="moe_combine_norm",
    )(pos, x, y_sorted, g.reshape(1, d))


MOE_ROW_TILE_MAX = 512


def _moe_row_tile(n_assign, n_experts):
    tm = 2 * SUBLANE
    while tm * 2 <= min(MOE_ROW_TILE_MAX, n_assign // (2 * n_experts)):
        tm *= 2
    return tm


def _moe_plan(top_idx, top_gate, n_experts, tm):
    r, top_k = top_idx.shape
    n_assign = r * top_k
    n_tiles = (n_assign + n_experts * (tm - 1)) // tm
    flat_e = top_idx.reshape(-1)
    counts = jnp.sum(flat_e[:, None] == jnp.arange(n_experts, dtype=jnp.int32)[None, :], axis=0).astype(jnp.int32)
    tiles_per = (counts + tm - 1) // tm
    tile_end = jnp.cumsum(tiles_per)
    tile_start = tile_end - tiles_per
    order = jnp.argsort(flat_e, stable=True).astype(jnp.int32)
    sorted_e = flat_e[order]
    count_start = jnp.cumsum(counts) - counts
    rank = jnp.arange(n_assign, dtype=jnp.int32) - count_start[sorted_e]
    pos_sorted = tile_start[sorted_e] * tm + rank
    pos = jnp.zeros((n_assign,), jnp.int32).at[order].set(pos_sorted)
    src_rows = jnp.zeros((n_tiles * tm,), jnp.int32).at[pos].set(jnp.arange(n_assign, dtype=jnp.int32) // top_k)
    row_gate = jnp.zeros((n_tiles * tm,), F32).at[pos].set(top_gate.reshape(-1))
    total = tile_end[-1]
    t = jnp.arange(n_tiles, dtype=jnp.int32)
    valid = t < total
    t_eff = jnp.minimum(t, total - 1)
    te = jnp.minimum(jnp.searchsorted(tile_end, t_eff, side="right").astype(jnp.int32), n_experts - 1)
    first = valid & (t == tile_start[te])
    tiles = (te, first.astype(jnp.int32), valid.astype(jnp.int32), t)
    return tiles, pos, src_rows, row_gate.reshape(-1, 1)


def _prev_rows(rows3):
    return jnp.pad(rows3, ((0, 0), (SUBLANE - (CONV_W - 1), 0), (0, 0)))


def _last_rows(arr, col0, c, row0, nb, seq):
    blk = lax.slice(arr, (row0, col0), (row0 + nb * seq, col0 + c)).reshape(nb, seq, c)
    return blk[:, seq - (CONV_W - 1):, :]


def kernel(x_prompt, x_sample, state_ssd_conv, state_ssd, state_mlstm_conv, state_mlstm_c, state_mlstm_n,
           state_mlstm_m, state_gla, meta_tokens,
           norm_mix_even, w_in_even, ssd_conv_w, ssd_conv_b, ssd_dt_bias, ssd_a_log, ssd_d, ssd_norm,
           mlstm_conv_w, mlstm_conv_b, mlstm_if_bias, mlstm_norm, w_out_even,
           norm_ffn_even, ffn_w_gate, ffn_w_up, ffn_w_down,
           norm_mix_odd, w_in_odd, gla_w_gate_up, gla_b_gate, gla_norm, w_out_odd,
           norm_ffn_odd, moe_router, moe_w1, moe_w3, moe_w2, final_norm):
    bp, sp, d = x_prompt.shape
    bs, ss, _ = x_sample.shape
    n_meta = meta_tokens.shape[0]
    assert w_in_even.shape[0] == 1 and w_in_odd.shape[0] == 1, "one even and one odd layer"

    ssd_heads = ssd_a_log.shape[-1]
    d_inner = ssd_norm.shape[-1]
    p_dim = d_inner // ssd_heads
    conv_dim = ssd_conv_w.shape[-1]
    n_state = (conv_dim - d_inner) // (2 * SSD_GROUPS)
    hpg = ssd_heads // SSD_GROUPS
    ml_heads = mlstm_if_bias.shape[-1] // 2
    ml_v = mlstm_norm.shape[-1]
    ml_dv = ml_v // ml_heads
    ml_qk = mlstm_conv_w.shape[-1] // 2
    ml_dk = ml_qk // ml_heads
    gla_heads, gla_dk, gla_dv = state_gla.shape[2:]
    gla_rank = gla_w_gate_up.shape[1]
    n_experts = moe_router.shape[-1]
    top_k = 2

    n_p, n_s, n_m = bp * sp, bs * ss, bp * n_meta
    rows = n_p + n_s + n_m
    row_p, row_s, row_m = 0, n_p, n_p + n_s
    q_p = math.gcd(sp, CHUNK_PROMPT)
    seg_m = (row_m, bp, n_meta, n_meta)
    seg_p = (row_p, bp, sp, q_p)
    seg_s = (row_s, bs, ss, ss)

    meta_rows = jnp.broadcast_to(meta_tokens.astype(F32)[None], (bp, n_meta, d)).reshape(n_m, d)
    x0 = jnp.concatenate([x_prompt.reshape(n_p, d), x_sample.reshape(n_s, d), meta_rows], axis=0)


    def assemble(parts):
        m_, p_, s_ = parts
        return jnp.concatenate([p_, s_, m_], axis=0)

    xn = _rmsnorm(x0, norm_mix_even[0])
    w_in = w_in_even
    off_z, off_xbc = 0, d_inner
    off_dt = off_xbc + conv_dim
    off_qk = off_dt + ssd_heads
    off_v = off_qk + 2 * ml_qk
    off_if = off_v + ml_v
    off_o = off_if + 2 * ml_heads
    n_a = off_dt
    xw_a = _dense_mm(xn, w_in, n_cols=n_a, col0=0, name="in_even_a")
    w_b = jnp.concatenate([w_in[:, :, off_qk:off_if], w_in[:, :, off_o:off_o + ml_v]], axis=2)
    xw_b = _dense_mm(xn, w_b, name="in_even_b")
    n_small = ssd_heads + 2 * ml_heads
    w_s = jnp.concatenate([w_in[:, :, off_dt:off_qk], w_in[:, :, off_if:off_o]], axis=2)
    w_s = jnp.pad(w_s, ((0, 0), (0, 0), (0, LANE - n_small)))
    small = _dense_mm(xn, w_s, name="in_even_small")

    dt_g = small[:, :ssd_heads].reshape(rows, SSD_GROUPS, hpg).transpose(1, 0, 2)
    par_g = jnp.stack([ssd_dt_bias[0], ssd_a_log[0], ssd_d[0]], axis=0).reshape(3, SSD_GROUPS, hpg)
    par_g = par_g.transpose(1, 0, 2)
    gt_h = small[:, ssd_heads:n_small].reshape(rows, 2, ml_heads).transpose(2, 0, 1)
    gb_h = mlstm_if_bias[0].reshape(2, ml_heads).transpose(1, 0).reshape(ml_heads, 1, 2)
    ssd_gn = ssd_norm[0].reshape(1, d_inner)
    ml_gn = mlstm_norm[0].reshape(1, ml_v)

    def even_segment(seg, prev_ssd, prev_ml, st):
        row0, nb, seq, q = seg
        xbc_c = _conv_silu(xw_a, off_xbc, conv_dim, row0, nb, seq, _prev_rows(prev_ssd),
                           ssd_conv_w[0], ssd_conv_b[0], "ssd_conv")
        qk_c = _conv_silu(xw_b, 0, 2 * ml_qk, row0, nb, seq, _prev_rows(prev_ml),
                          mlstm_conv_w[0], mlstm_conv_b[0], "mlstm_conv")
        h0 = None if st is None else st[0]
        y_ssd, h_new = _ssd(xbc_c, xw_a, off_z, dt_g, par_g, ssd_gn, h0, row0, nb, seq, q, hpg, p_dim,
                            n_state, "ssd_scan")
        ml0 = None if st is None else st[1:]
        y_ml, c_new, n_new, m_new = _mlstm(qk_c, xw_b, 2 * ml_qk, 2 * ml_qk + ml_v, gt_h, gb_h, ml_gn, ml0,
                                           row0, nb, seq, q, ml_heads, ml_dk, ml_dv, "mlstm_scan")
        new_conv_ssd = _last_rows(xw_a, off_xbc, conv_dim, row0, nb, seq)
        new_conv_ml = _last_rows(xw_b, 0, 2 * ml_qk, row0, nb, seq)
        mix = jnp.concatenate([y_ssd, y_ml], axis=1)
        return mix, (h_new, c_new, n_new, m_new), new_conv_ssd, new_conv_ml

    zeros_ssd = jnp.zeros((bp, CONV_W - 1, conv_dim), F32)
    zeros_ml = jnp.zeros((bp, CONV_W - 1, 2 * ml_qk), F32)
    mix_m, st_m, cs_m, cm_m = even_segment(seg_m, zeros_ssd, zeros_ml, None)
    mix_p, st_p, cs_p, cm_p = even_segment(seg_p, cs_m, cm_m, st_m)
    st_s0 = (state_ssd[0].reshape(bs, SSD_GROUPS, hpg * p_dim, n_state), state_mlstm_c[0],
             state_mlstm_n[0].reshape(bs, ml_heads, 1, ml_dk), state_mlstm_m[0].reshape(bs, ml_heads, 1, 1))
    mix_s, st_s, cs_s, cm_s = even_segment(seg_s, state_ssd_conv[0], state_mlstm_conv[0], st_s0)
    mix = assemble((mix_m, mix_p, mix_s))

    x1 = _dense_mm(mix, w_out_even, res=x0, name="out_even")
    xn = _rmsnorm(x1, norm_ffn_even[0])
    hid = _dense_swiglu(xn, ffn_w_gate, ffn_w_up, "ffn_up")
    x2 = _dense_mm(hid, ffn_w_down, res=x1, name="ffn_down")

    xn = _rmsnorm(x2, norm_mix_odd[0])
    n_main = 2 * gla_heads * gla_dk + 2 * gla_heads * gla_dv
    xw_o = _dense_mm(xn, w_in_odd, n_cols=n_main, col0=0, name="in_odd")
    w_ga = jnp.pad(w_in_odd[:, :, n_main:n_main + gla_rank], ((0, 0), (0, 0), (0, LANE - gla_rank)))
    small_o = _dense_mm(xn, w_ga, name="in_odd_small")
    gla_gn = gla_norm[0].reshape(1, gla_heads * gla_dv)
    bg = gla_b_gate[0].reshape(1, gla_heads * gla_dk)
    wgu_pad = jnp.pad(gla_w_gate_up[0], ((0, LANE - gla_rank), (0, 0)))
    k_col0 = gla_heads * gla_dk
    v_col0 = 2 * gla_heads * gla_dk
    r_col0 = v_col0 + gla_heads * gla_dv

    def odd_segment(seg, s0):
        row0, nb, seq, q = seg
        return _gla(xw_o, 0, k_col0, v_col0, r_col0, small_o, wgu_pad, bg, gla_gn, s0, row0, nb, seq,
                    q, gla_heads, gla_dk, gla_dv, "gla_scan")

    o_m, s_m = odd_segment(seg_m, None)
    o_p, s_p = odd_segment(seg_p, s_m)
    o_s, s_s = odd_segment(seg_s, state_gla[0])
    mix = assemble((o_m, o_p, o_s))
    x3 = _dense_mm(mix, w_out_odd, res=x2, name="out_odd")

    idx_pad, gate_pad = _router(x3, norm_ffn_odd[0], moe_router[0])
    tm_e = _moe_row_tile(rows * top_k, n_experts)
    tiles_e, pos, src_rows, row_gate = _moe_plan(idx_pad[:, :top_k], gate_pad[:, :top_k], n_experts, tm_e)
    xs = _gather_norm(x3, norm_ffn_odd[0], src_rows, _pick(tm_e, (256, 128, 64, 32, 16, 8)))
    _, tn_up = _plan_swiglu(d, moe_w1.shape[-1], (tm_e,))
    hs = _gmm_swiglu(xs, moe_w1[0], moe_w3[0], tiles_e, tm=tm_e, tn=tn_up, name="moe_up")
    _, tn_dn, tk_dn = _plan_mm(moe_w2.shape[-2], d, 0, (tm_e,), False)
    ys = _gmm(hs, moe_w2[0], tiles_e, tm=tm_e, tn=tn_dn, tk=tk_dn, scale=row_gate, name="moe_down")
    y = _combine_norm(x3, ys, pos, final_norm, top_k)

    y_prompt = y[row_p:row_p + n_p].reshape(bp, sp, d)
    y_sample = y[row_s:row_s + n_s].reshape(bs, ss, d)

    def pack_even(cs, cm, st, nb):
        h_new, c_new, n_new, m_new = st
        return (cs[None], h_new.reshape(1, nb, ssd_heads, p_dim, n_state), cm[None], c_new[None],
                n_new.reshape(1, nb, ml_heads, ml_dk), m_new.reshape(1, nb, ml_heads))

    return ((y_prompt, y_sample) + pack_even(cs_p, cm_p, st_p, bp) + (s_p[None],)
            + pack_even(cs_s, cm_s, st_s, bs) + (s_s[None],))
```

```python
import functools
import math

import jax
import jax.numpy as jnp
from jax import lax
from jax.experimental import pallas as pl
from jax.experimental.pallas import tpu as pltpu

F32 = jnp.float32
BF16 = jnp.bfloat16

EPS = 1e-6
CONV_W = 4
SSD_GROUPS = 8
GLA_TAU = 16.0
N_EXPERTS_PAD = 128
PROMPT_CHUNKS = (128, 256, 64)
SSD_BLOCKING = (4, 2)
MLSTM_BLOCKING = (4, 2)
GLA_BLOCKING = (1, 2)

V7X_VMEM_BYTES = 64 * 1024 * 1024
VMEM_LIMIT = V7X_VMEM_BYTES - 8 * 1024 * 1024
VMEM_BUDGET = VMEM_LIMIT - 4 * 1024 * 1024
LANE = 128
SUBLANE = 8
TK_MAX = 4096


def _cparams(n_axes):
    return pltpu.CompilerParams(dimension_semantics=("arbitrary",) * n_axes, vmem_limit_bytes=VMEM_LIMIT)


def _pick(n, candidates):
    for c in candidates:
        if c <= n and n % c == 0:
            return c
    return n


def _row_tile(n, cap):
    if n <= cap:
        return n
    best = SUBLANE
    for c in range(SUBLANE, cap + 1, SUBLANE):
        if n % c == 0:
            best = c
    return best


def _dot(a, b, dims=((1,), (0,))):
    return lax.dot_general(a.astype(BF16), b.astype(BF16), (dims, ((), ())), preferred_element_type=F32)


def _split3(x):
    hi = x.astype(BF16)
    r1 = x - hi.astype(F32)
    mid = r1.astype(BF16)
    lo = (r1 - mid.astype(F32)).astype(BF16)
    return hi, mid, lo


def _dot_exact_rhs(a01, b, dims=((1,), (0,))):
    a = a01.astype(BF16)
    out = None
    for part in _split3(b):
        t = lax.dot_general(a, part, (dims, ((), ())), preferred_element_type=F32)
        out = t if out is None else out + t
    return out


def _dot_exact_lhs(a, b01, dims=((1,), (0,))):
    b = b01.astype(BF16)
    out = None
    for part in _split3(a):
        t = lax.dot_general(part, b, (dims, ((), ())), preferred_element_type=F32)
        out = t if out is None else out + t
    return out


def _iota2(shape, axis):
    return lax.broadcasted_iota(jnp.int32, shape, axis)


def _tril(q):
    return _iota2((q, q), 0) >= _iota2((q, q), 1)


def _transpose_small(x, q):
    eye = (_iota2((q, q), 0) == _iota2((q, q), 1)).astype(F32)
    return _dot_exact_lhs(x, eye, dims=((0,), (0,)))


def _silu(x):
    return x * jax.nn.sigmoid(x)


def _rms(x, g):
    return x * lax.rsqrt(jnp.mean(x * x, axis=-1, keepdims=True) + EPS) * g


def _rmsnorm_body(x_ref, g_ref, o_ref):
    o_ref[...] = _rms(x_ref[...], g_ref[...]).astype(o_ref.dtype)


def _rmsnorm(x, g, out_dtype=BF16):
    r, d = x.shape
    tr = _row_tile(r, 256)
    return pl.pallas_call(
        _rmsnorm_body,
        out_shape=jax.ShapeDtypeStruct((r, d), out_dtype),
        grid=(r // tr,),
        in_specs=[pl.BlockSpec((tr, d), lambda i: (i, 0)), pl.BlockSpec((1, d), lambda i: (0, 0))],
        out_specs=pl.BlockSpec((tr, d), lambda i: (i, 0)),
        compiler_params=_cparams(1),
        name="rmsnorm",
    )(x, g.reshape(1, d))


def _gmm_body(te_ref, tf_ref, tv_ref, tr_ref, x_ref, w_ref, *rest, kc, has_res, has_scale):
    del te_ref, tr_ref
    rest = list(rest)
    res_ref = rest.pop(0) if has_res else None
    scale_ref = rest.pop(0) if has_scale else None
    o_ref, wbf_ref = rest[0], rest[1]
    acc_ref = rest[2] if kc > 1 else None
    t = pl.program_id(1)
    c = pl.program_id(2)

    @pl.when(tf_ref[t] == 1)
    def _cast():
        wbf_ref[c] = w_ref[...].astype(BF16)

    def finish(v):
        if has_scale:
            v = v * scale_ref[...]
        if has_res:
            v = v + res_ref[...]
        o_ref[...] = v.astype(o_ref.dtype)

    @pl.when(tv_ref[t] == 1)
    def _compute():
        part = jnp.dot(x_ref[...], wbf_ref[c], preferred_element_type=F32)
        if kc == 1:
            finish(part)
        else:
            @pl.when(c == 0)
            def _():
                acc_ref[...] = part

            @pl.when(c > 0)
            def _():
                acc_ref[...] += part

            @pl.when(c == kc - 1)
            def _():
                finish(acc_ref[...])

    @pl.when(tv_ref[t] == 0)
    def _pad():
        o_ref[...] = jnp.zeros_like(o_ref)


def _dense_tiles(n_tiles):
    t = jnp.arange(n_tiles, dtype=jnp.int32)
    return (jnp.zeros((n_tiles,), jnp.int32), (t == 0).astype(jnp.int32), jnp.ones((n_tiles,), jnp.int32), t)


def _tk_options(k):
    if k <= TK_MAX:
        return [k]
    divs = [c for c in range(LANE, k, LANE) if k % c == 0]
    good = sorted((c for c in divs if TK_MAX // 2 <= c <= TK_MAX), reverse=True)
    over = sorted(c for c in divs if TK_MAX < c <= 2 * TK_MAX)
    return good + over or sorted(divs, reverse=True)


def _tm_options(rows):
    opts = [c for c in range(2 * SUBLANE, min(rows, 1024) + 1, 2 * SUBLANE) if rows % c == 0]
    return sorted(opts, reverse=True) or [rows]


def _plan_mm(k, n_cols, col0, tm_options, has_res):
    for tn in (512, 256, 128):
        if n_cols % tn or col0 % tn:
            continue
        for tm in tm_options:
            for tk in _tk_options(k):
                kc = k // tk
                need = (2 * tk * tn * 4 + k * tn * 2 + 2 * tm * tk * 2 + 3 * tm * tn * 4
                        + (2 * tm * tn * 4 if has_res else 0) + (tm * tn * 4 if kc > 1 else 0))
                if need <= VMEM_BUDGET:
                    return tm, tn, tk
    raise ValueError(f"no matmul tiling fits VMEM for k={k} n={n_cols}")


def _plan_swiglu(k, f, tm_options):
    for tn in (512, 256, 128):
        if f % tn:
            continue
        for tm in tm_options:
            need = 2 * (2 * k * tn * 4) + 2 * k * tn * 2 + 2 * tm * k * 2 + 2 * tm * tn * 2 + 3 * tm * tn * 4
            if need <= VMEM_BUDGET:
                return tm, tn
    raise ValueError(f"no swiglu tiling fits VMEM for k={k} f={f}")


def _gmm(x, w, tiles, *, tm, tn, tk, n_cols=None, col0=0, res=None, scale=None, out_dtype=F32, name="gmm"):
    te, tf, tv, tr = tiles
    n_tiles = te.shape[0]
    rows, k = x.shape
    n_total = w.shape[2]
    n_cols = n_total if n_cols is None else n_cols
    assert col0 % tn == 0 and n_cols % tn == 0 and rows % tm == 0 and k % tk == 0
    kc = k // tk
    j0 = col0 // tn

    def x_map(j, t, c, te, tf, tv, tr):
        return (tr[t], jnp.where(tv[t] == 1, c, kc - 1))

    def w_map(j, t, c, te, tf, tv, tr):
        return (te[t], jnp.where(tf[t] == 1, c, kc - 1), j0 + j)

    def o_map(j, t, c, te, tf, tv, tr):
        return (tr[t], j)

    in_specs = [pl.BlockSpec((tm, tk), x_map), pl.BlockSpec((None, tk, tn), w_map)]
    args = [x, w]
    if res is not None:
        in_specs.append(pl.BlockSpec((tm, tn), o_map))
        args.append(res)
    if scale is not None:
        in_specs.append(pl.BlockSpec((tm, 1), lambda j, t, c, te, tf, tv, tr: (tr[t], 0)))
        args.append(scale)
    scratch = [pltpu.VMEM((kc, tk, tn), BF16)]
    if kc > 1:
        scratch.append(pltpu.VMEM((tm, tn), F32))
    return pl.pallas_call(
        functools.partial(_gmm_body, kc=kc, has_res=res is not None, has_scale=scale is not None),
        out_shape=jax.ShapeDtypeStruct((rows, n_cols), out_dtype),
        grid_spec=pltpu.PrefetchScalarGridSpec(
            num_scalar_prefetch=4,
            grid=(n_cols // tn, n_tiles, kc),
            in_specs=in_specs,
            out_specs=pl.BlockSpec((tm, tn), o_map),
            scratch_shapes=scratch,
        ),
        compiler_params=_cparams(3),
        name=name,
    )(te, tf, tv, tr, *args)


def _gmm_swiglu_body(te_ref, tf_ref, tv_ref, tr_ref, x_ref, wg_ref, wu_ref, o_ref, wgb_ref, wub_ref):
    del te_ref, tr_ref
    t = pl.program_id(1)

    @pl.when(tf_ref[t] == 1)
    def _cast():
        wgb_ref[...] = wg_ref[...].astype(BF16)
        wub_ref[...] = wu_ref[...].astype(BF16)

    @pl.when(tv_ref[t] == 1)
    def _compute():
        x = x_ref[...]
        g = jnp.dot(x, wgb_ref[...], preferred_element_type=F32)
        u = jnp.dot(x, wub_ref[...], preferred_element_type=F32)
        o_ref[...] = (_silu(g) * u).astype(o_ref.dtype)

    @pl.when(tv_ref[t] == 0)
    def _pad():
        o_ref[...] = jnp.zeros_like(o_ref)


def _gmm_swiglu(x, wg, wu, tiles, *, tm, tn, name="gmm_swiglu"):
    te, tf, tv, tr = tiles
    n_tiles = te.shape[0]
    rows, k = x.shape
    f = wg.shape[2]
    assert rows % tm == 0 and f % tn == 0

    def x_map(j, t, te, tf, tv, tr):
        return (tr[t], 0)

    def w_map(j, t, te, tf, tv, tr):
        return (te[t], 0, j)

    def o_map(j, t, te, tf, tv, tr):
        return (tr[t], j)

    return pl.pallas_call(
        _gmm_swiglu_body,
        out_shape=jax.ShapeDtypeStruct((rows, f), BF16),
        grid_spec=pltpu.PrefetchScalarGridSpec(
            num_scalar_prefetch=4,
            grid=(f // tn, n_tiles),
            in_specs=[pl.BlockSpec((tm, k), x_map), pl.BlockSpec((None, k, tn), w_map),
                      pl.BlockSpec((None, k, tn), w_map)],
            out_specs=pl.BlockSpec((tm, tn), o_map),
            scratch_shapes=[pltpu.VMEM((k, tn), BF16), pltpu.VMEM((k, tn), BF16)],
        ),
        compiler_params=_cparams(2),
        name=name,
    )(te, tf, tv, tr, x, wg, wu)


def _dense_mm(x, w, *, n_cols=None, col0=0, res=None, name):
    rows, k = x.shape
    n = w.shape[2] if n_cols is None else n_cols
    tm, tn, tk = _plan_mm(k, n, col0, _tm_options(rows), res is not None)
    return _gmm(x, w, _dense_tiles(rows // tm), tm=tm, tn=tn, tk=tk, n_cols=n_cols, col0=col0, res=res, name=name)


def _dense_swiglu(x, wg, wu, name):
    rows, k = x.shape
    tm, tn = _plan_swiglu(k, wg.shape[2], _tm_options(rows))
    return _gmm_swiglu(x, wg, wu, _dense_tiles(rows // tm), tm=tm, tn=tn, name=name)


def _conv_body(prev_ref, u_ref, w_ref, b_ref, o_ref, full_ref, *, seq, bb):
    for bi in range(bb):
        r0 = bi * seq
        f0 = bi * (SUBLANE + seq)
        full_ref[f0:f0 + SUBLANE, :] = prev_ref[bi]
        full_ref[f0 + SUBLANE:f0 + SUBLANE + seq, :] = u_ref[r0:r0 + seq, :]
        acc = b_ref[...]
        for k in range(CONV_W):
            start = f0 + SUBLANE - (CONV_W - 1) + k
            acc = acc + full_ref[start:start + seq, :] * w_ref[k:k + 1, :]
        o_ref[r0:r0 + seq, :] = _silu(acc)


CONV_BLOCK_BYTES = 2 * 1024 * 1024


def _conv_silu(u_arr, col0, c, row0, nb, seq, prev, w, b, name):
    ct = LANE
    for cand in (2048, 1024, 512, 256):
        if c % cand == 0 and col0 % cand == 0 and seq * cand * 4 <= CONV_BLOCK_BYTES:
            ct = cand
            break
    bb = 1
    while bb * 2 <= 16 and nb % (bb * 2) == 0 and bb * 2 * seq * ct * 4 <= CONV_BLOCK_BYTES:
        bb *= 2
    assert col0 % ct == 0 and row0 % (bb * seq) == 0
    rb0, cb0 = row0 // (bb * seq), col0 // ct
    return pl.pallas_call(
        functools.partial(_conv_body, seq=seq, bb=bb),
        out_shape=jax.ShapeDtypeStruct((nb * seq, c), F32),
        grid=(nb // bb, c // ct),
        in_specs=[pl.BlockSpec((bb, SUBLANE, ct), lambda b_, j: (b_, 0, j)),
                  pl.BlockSpec((bb * seq, ct), lambda b_, j: (rb0 + b_, cb0 + j)),
                  pl.BlockSpec((CONV_W, ct), lambda b_, j: (0, j)),
                  pl.BlockSpec((1, ct), lambda b_, j: (0, j))],
        out_specs=pl.BlockSpec((bb * seq, ct), lambda b_, j: (b_, j)),
        scratch_shapes=[pltpu.VMEM((bb * (SUBLANE + seq), ct), F32)],
        compiler_params=_cparams(2),
        name=name,
    )(prev, u_arr, w, b.reshape(1, c))


def _scan_blocking(nb, units, nc, bb_max, ub_max):
    bb = 1
    if nc == 1:
        while bb * 2 <= bb_max and nb % (bb * 2) == 0:
            bb *= 2
    ub = 1
    while ub * 2 <= ub_max and units % (ub * 2) == 0:
        ub *= 2
    return bb, ub


def _init_carried(carry, zero_init, pairs):
    if not carry:
        return

    @pl.when(pl.program_id(2) == 0)
    def _init():
        for out_ref, in_ref in pairs:
            out_ref[...] = jnp.zeros_like(out_ref) if zero_init else in_ref[...]


def _read_state(carry, zero_init, out_ref, in_ref, bi, ui):
    if carry:
        return out_ref[bi, ui]
    if zero_init:
        return jnp.zeros(out_ref.shape[2:], out_ref.dtype)
    return in_ref[bi, ui]


def _ssd_body(*refs, q, hpg, p, bb, gb, zero_init, carry):
    refs = list(refs)
    x_ref, b_ref, c_ref, z_ref, dt_ref, par_ref, gn_ref = refs[:7]
    pos = 7
    h0_ref = None
    if not zero_init:
        h0_ref = refs[pos]
        pos += 1
    y_ref, h_ref, ys_ref = refs[pos + 1:pos + 4]
    _init_carried(carry, zero_init, [(h_ref, h0_ref)])

    gw = hpg * p
    n = b_ref.shape[1] // gb
    tril = _tril(q)
    trilf = tril.astype(F32)
    expand = (_iota2((hpg, gw), 1) // p == _iota2((hpg, gw), 0)).astype(F32)
    expand_t = (_iota2((gw, hpg), 0) // p == _iota2((gw, hpg), 1)).astype(F32)
    for bi in range(bb):
        r0 = bi * q
        for gi in range(gb):
            c0 = gi * gw
            x = x_ref[r0:r0 + q, c0:c0 + gw]
            bm = b_ref[r0:r0 + q, gi * n:(gi + 1) * n]
            cm = c_ref[r0:r0 + q, gi * n:(gi + 1) * n]
            par = par_ref[gi]
            dt = jax.nn.softplus(dt_ref[gi, r0:r0 + q, :] + par[0:1, :])
            a = -jnp.exp(par[1:2, :])
            dta = dt * a
            cum = _dot_exact_rhs(trilf, dta)
            cum_t = _transpose_small(cum, q)
            dt_t = _transpose_small(dt, q)
            hstate = _read_state(carry, zero_init, h_ref, h0_ref, bi, gi)
            cb = _dot(cm, bm, dims=((1,), (1,)))
            y_inter = _dot(cm, hstate, dims=((1,), (1,)))
            cum_last = cum[q - 1:q, :]
            w_end = jnp.exp(cum_last - cum) * dt
            ecum_w = _dot_exact_lhs(jnp.exp(cum), expand)
            wend_w = _dot_exact_lhs(w_end, expand)
            d_w = _dot_exact_lhs(par[2:3, :], expand)
            for h in range(hpg):
                seg = cum[:, h:h + 1] - cum_t[h:h + 1, :]
                lmat = jnp.exp(jnp.where(tril, seg, -jnp.inf))
                scores = cb * lmat * dt_t[h:h + 1, :]
                ys_ref[r0:r0 + q, c0 + h * p:c0 + (h + 1) * p] = _dot(scores, x[:, h * p:(h + 1) * p])
            y = ys_ref[r0:r0 + q, c0:c0 + gw] + y_inter * ecum_w + d_w * x
            y = y * _silu(z_ref[r0:r0 + q, c0:c0 + gw])
            y_ref[r0:r0 + q, c0:c0 + gw] = _rms(y, gn_ref[:, c0:c0 + gw]).astype(y_ref.dtype)
            last_col = jnp.broadcast_to(cum_t[:, q - 1:q], (hpg, LANE))
            decay = jnp.exp(_dot_exact_rhs(expand_t, last_col))
            if n != LANE:
                decay = jnp.broadcast_to(decay[:, 0:1], hstate.shape)
            h_ref[bi, gi] = decay * hstate + _dot(x * wend_w, bm, dims=((0,), (0,)))


def _ssd(xbc_c, xw_a, z_col0, dt_g, par_g, gnorm, h0, mix, y_col0, row0, nb, seq, q, hpg, p, n, blocking, name):
    g_cnt = SSD_GROUPS
    nc = seq // q
    gw = hpg * p
    d_inner = g_cnt * gw
    bb, gb = _scan_blocking(nb, g_cnt, nc, *blocking)
    rq, cw = bb * q, gb * gw
    assert row0 % rq == 0 and z_col0 % cw == 0 and y_col0 % cw == 0 and d_inner % (gb * n) == 0
    rb0 = row0 // rq
    zb0, yb0 = z_col0 // cw, y_col0 // cw
    bb0 = d_inner // (gb * n)
    cb0 = (d_inner + g_cnt * n) // (gb * n)
    zero_init = h0 is None

    in_specs = [
        pl.BlockSpec((rq, cw), lambda b, g, c: (b * nc + c, g)),
        pl.BlockSpec((rq, gb * n), lambda b, g, c: (b * nc + c, bb0 + g)),
        pl.BlockSpec((rq, gb * n), lambda b, g, c: (b * nc + c, cb0 + g)),
        pl.BlockSpec((rq, cw), lambda b, g, c: (rb0 + b * nc + c, zb0 + g)),
        pl.BlockSpec((gb, rq, hpg), lambda b, g, c: (g, rb0 + b * nc + c, 0)),
        pl.BlockSpec((gb, 3, hpg), lambda b, g, c: (g, 0, 0)),
        pl.BlockSpec((1, cw), lambda b, g, c: (0, g)),
    ]
    args = [xbc_c, xbc_c, xbc_c, xw_a, dt_g, par_g, gnorm]
    st_spec = pl.BlockSpec((bb, gb, gw, n), lambda b, g, c: (b, g, 0, 0))
    if not zero_init:
        in_specs.append(st_spec)
        args.append(h0)
    in_specs.append(pl.BlockSpec(memory_space=pl.ANY))
    args.append(mix)
    return pl.pallas_call(
        functools.partial(_ssd_body, q=q, hpg=hpg, p=p, bb=bb, gb=gb, zero_init=zero_init, carry=nc > 1),
        out_shape=(jax.ShapeDtypeStruct(mix.shape, mix.dtype),
                   jax.ShapeDtypeStruct((nb, g_cnt, gw, n), F32)),
        grid=(nb // bb, g_cnt // gb, nc),
        in_specs=in_specs,
        out_specs=(pl.BlockSpec((rq, cw), lambda b, g, c: (rb0 + b * nc + c, yb0 + g)), st_spec),
        scratch_shapes=[pltpu.VMEM((rq, cw), F32)],
        input_output_aliases={len(args) - 1: 0},
        compiler_params=_cparams(3),
        name=name,
    )(*args)


def _mlstm_body(*refs, q, dk, dv, bb, hb, zero_init, carry):
    refs = list(refs)
    q_ref, k_ref, v_ref, o_ref, gt_ref, gb_ref, gn_ref = refs[:7]
    pos = 7
    c0_ref = n0_ref = m0_ref = None
    if not zero_init:
        c0_ref, n0_ref, m0_ref = refs[pos:pos + 3]
        pos += 3
    y_ref, c_ref, n_ref, m_ref = refs[pos + 1:pos + 5]
    _init_carried(carry, zero_init, [(c_ref, c0_ref), (n_ref, n0_ref), (m_ref, m0_ref)])

    tril = _tril(q)
    trilf = tril.astype(F32)
    lane0 = _iota2((q, LANE), 1) == 0
    for bi in range(bb):
        r0 = bi * q
        for hi in range(hb):
            qh = q_ref[r0:r0 + q, hi * dk:(hi + 1) * dk] * (dk ** -0.5)
            kh = k_ref[r0:r0 + q, hi * dk:(hi + 1) * dk]
            vh = v_ref[r0:r0 + q, hi * dv:(hi + 1) * dv]
            gates = gt_ref[hi, r0:r0 + q, :] + gb_ref[hi]
            log_i = gates[:, 0:1]
            log_f = jax.nn.log_sigmoid(gates[:, 1:2])
            cumf = _dot_exact_rhs(trilf, jnp.broadcast_to(log_f, (q, LANE)))[:, 0:1]
            rows = _transpose_small(jnp.where(lane0, cumf, log_i), q)
            cumf_r, logi_r = rows[0:1, :], rows[1:2, :]
            m_prev = _read_state(carry, zero_init, m_ref, m0_ref, bi, hi)
            cmat = _read_state(carry, zero_init, c_ref, c0_ref, bi, hi)
            nvec = _read_state(carry, zero_init, n_ref, n0_ref, bi, hi)
            dmat = jnp.where(tril, cumf - cumf_r + logi_r, -jnp.inf)
            from_state = cumf + m_prev
            m_row = jnp.maximum(jnp.max(dmat, axis=1, keepdims=True), from_state)
            w = jnp.exp(dmat - m_row)
            s_state = jnp.exp(from_state - m_row)
            qk = _dot(qh, kh, dims=((1,), (1,))) * w
            num = _dot(qk, vh) + _dot(qh, cmat) * s_state
            den = jnp.sum(qk, axis=1, keepdims=True) + jnp.sum(qh * nvec, axis=1, keepdims=True) * s_state
            den = jnp.maximum(jnp.abs(den), jnp.exp(-m_row))
            hout = num / den
            m_new = m_row[q - 1:q, :]
            cumf_last = cumf[q - 1:q, :]
            w_end = jnp.exp(cumf_last - cumf + log_i - m_new)
            s_end = jnp.exp(cumf_last + m_prev - m_new)
            kw = kh * w_end
            c_ref[bi, hi] = s_end * cmat + _dot(kw, vh, dims=((0,), (0,)))
            n_ref[bi, hi] = s_end * nvec + jnp.sum(kw, axis=0, keepdims=True)
            m_ref[bi, hi] = m_new
            gate_o = jax.nn.sigmoid(o_ref[r0:r0 + q, hi * dv:(hi + 1) * dv])
            y_ref[r0:r0 + q, hi * dv:(hi + 1) * dv] = (
                _rms(hout, gn_ref[:, hi * dv:(hi + 1) * dv]) * gate_o).astype(y_ref.dtype)


def _mlstm(qk_c, xw_b, v_col0, o_col0, gt_h, gb_h, gnorm, state0, mix, y_col0, row0, nb, seq, q, heads, dk, dv,
           blocking, name):
    nc = seq // q
    bb, hb = _scan_blocking(nb, heads, nc, *blocking)
    rq = bb * q
    assert row0 % rq == 0 and v_col0 % (hb * dv) == 0 and o_col0 % (hb * dv) == 0 and y_col0 % (hb * dv) == 0
    rb0 = row0 // rq
    vb0, ob0, yb0 = v_col0 // (hb * dv), o_col0 // (hb * dv), y_col0 // (hb * dv)
    kb0 = heads // hb
    zero_init = state0 is None
    in_specs = [
        pl.BlockSpec((rq, hb * dk), lambda b, h, c: (b * nc + c, h)),
        pl.BlockSpec((rq, hb * dk), lambda b, h, c: (b * nc + c, kb0 + h)),
        pl.BlockSpec((rq, hb * dv), lambda b, h, c: (rb0 + b * nc + c, vb0 + h)),
        pl.BlockSpec((rq, hb * dv), lambda b, h, c: (rb0 + b * nc + c, ob0 + h)),
        pl.BlockSpec((hb, rq, 2), lambda b, h, c: (h, rb0 + b * nc + c, 0)),
        pl.BlockSpec((hb, 1, 2), lambda b, h, c: (h, 0, 0)),
        pl.BlockSpec((1, hb * dv), lambda b, h, c: (0, h)),
    ]
    args = [qk_c, qk_c, xw_b, xw_b, gt_h, gb_h, gnorm]
    st_specs = (pl.BlockSpec((bb, hb, dk, dv), lambda b, h, c: (b, h, 0, 0)),
                pl.BlockSpec((bb, hb, 1, dk), lambda b, h, c: (b, h, 0, 0)),
                pl.BlockSpec((bb, hb, 1, 1), lambda b, h, c: (b, h, 0, 0)))
    if not zero_init:
        in_specs.extend(st_specs)
        args.extend(state0)
    in_specs.append(pl.BlockSpec(memory_space=pl.ANY))
    args.append(mix)
    return pl.pallas_call(
        functools.partial(_mlstm_body, q=q, dk=dk, dv=dv, bb=bb, hb=hb, zero_init=zero_init, carry=nc > 1),
        out_shape=(jax.ShapeDtypeStruct(mix.shape, mix.dtype),
                   jax.ShapeDtypeStruct((nb, heads, dk, dv), F32),
                   jax.ShapeDtypeStruct((nb, heads, 1, dk), F32),
                   jax.ShapeDtypeStruct((nb, heads, 1, 1), F32)),
        grid=(nb // bb, heads // hb, nc),
        in_specs=in_specs,
        out_specs=(pl.BlockSpec((rq, hb * dv), lambda b, h, c: (rb0 + b * nc + c, yb0 + h)),) + st_specs,
        input_output_aliases={len(args) - 1: 0},
        compiler_params=_cparams(3),
        name=name,
    )(*args)


def _gla_body(*refs, q, dk, dv, bb, hb, zero_init, carry):
    refs = list(refs)
    q_ref, k_ref, v_ref, r_ref, ga_ref, wgu_ref, bg_ref, gn_ref = refs[:8]
    pos = 8
    s0_ref = None
    if not zero_init:
        s0_ref = refs[pos]
        pos += 1
    y_ref, s_ref = refs[pos + 1:pos + 3]
    _init_carried(carry, zero_init, [(s_ref, s0_ref)])
    tril = _tril(q)
    trilf = tril.astype(F32)
    ones = jnp.ones((q, LANE), F32)
    for bi in range(bb):
        r0 = bi * q
        ga = ga_ref[r0:r0 + q, :]
        for hi in range(hb):
            kh = k_ref[r0:r0 + q, hi * dk:(hi + 1) * dk]
            vh = v_ref[r0:r0 + q, hi * dv:(hi + 1) * dv]
            gate_pre = _dot(ga, wgu_ref[:, hi * dk:(hi + 1) * dk]) + bg_ref[:, hi * dk:(hi + 1) * dk]
            log_a = jax.nn.log_sigmoid(gate_pre) / GLA_TAU
            bcum = _dot_exact_rhs(trilf, log_a)
            qt = q_ref[r0:r0 + q, hi * dk:(hi + 1) * dk] * (dk ** -0.5) * jnp.exp(bcum)
            kt = kh * jnp.exp(-bcum)
            att = jnp.where(tril, _dot(qt, kt, dims=((1,), (1,))), 0.0)
            smat = _read_state(carry, zero_init, s_ref, s0_ref, bi, hi)
            o = _dot(att, vh) + _dot(qt, smat)
            b_last = bcum[q - 1:q, :]
            kw = kh * jnp.exp(b_last - bcum)
            tot = _dot_exact_lhs(log_a, ones, dims=((0,), (0,)))
            decay = jnp.exp(tot[:, 0:1])
            s_ref[bi, hi] = decay * smat + _dot(kw, vh, dims=((0,), (0,)))
            gate_r = _silu(r_ref[r0:r0 + q, hi * dv:(hi + 1) * dv])
            y_ref[r0:r0 + q, hi * dv:(hi + 1) * dv] = (
                _rms(o, gn_ref[:, hi * dv:(hi + 1) * dv]) * gate_r).astype(y_ref.dtype)


def _gla(xw, q_col0, k_col0, v_col0, r_col0, small, wgu, bg, gnorm, s0, mix, row0, nb, seq, q, heads, dk, dv,
         blocking, name):
    nc = seq // q
    bb, hb = _scan_blocking(nb, heads, nc, *blocking)
    rq = bb * q
    kw_, vw_ = hb * dk, hb * dv
    assert row0 % rq == 0 and q_col0 % kw_ == 0 and k_col0 % kw_ == 0 and v_col0 % vw_ == 0 and r_col0 % vw_ == 0
    rb0 = row0 // rq
    qb0, kb0, vb0, rcb0 = q_col0 // kw_, k_col0 // kw_, v_col0 // vw_, r_col0 // vw_
    zero_init = s0 is None
    in_specs = [
        pl.BlockSpec((rq, kw_), lambda b, h, c: (rb0 + b * nc + c, qb0 + h)),
        pl.BlockSpec((rq, kw_), lambda b, h, c: (rb0 + b * nc + c, kb0 + h)),
        pl.BlockSpec((rq, vw_), lambda b, h, c: (rb0 + b * nc + c, vb0 + h)),
        pl.BlockSpec((rq, vw_), lambda b, h, c: (rb0 + b * nc + c, rcb0 + h)),
        pl.BlockSpec((rq, LANE), lambda b, h, c: (rb0 + b * nc + c, 0)),
        pl.BlockSpec((LANE, kw_), lambda b, h, c: (0, h)),
        pl.BlockSpec((1, kw_), lambda b, h, c: (0, h)),
        pl.BlockSpec((1, vw_), lambda b, h, c: (0, h)),
    ]
    args = [xw, xw, xw, xw, small, wgu, bg, gnorm]
    st_spec = pl.BlockSpec((bb, hb, dk, dv), lambda b, h, c: (b, h, 0, 0))
    if not zero_init:
        in_specs.append(st_spec)
        args.append(s0)
    in_specs.append(pl.BlockSpec(memory_space=pl.ANY))
    args.append(mix)
    return pl.pallas_call(
        functools.partial(_gla_body, q=q, dk=dk, dv=dv, bb=bb, hb=hb, zero_init=zero_init, carry=nc > 1),
        out_shape=(jax.ShapeDtypeStruct(mix.shape, mix.dtype),
                   jax.ShapeDtypeStruct((nb, heads, dk, dv), F32)),
        grid=(nb // bb, heads // hb, nc),
        in_specs=in_specs,
        out_specs=(pl.BlockSpec((rq, vw_), lambda b, h, c: (rb0 + b * nc + c, h)), st_spec),
        input_output_aliases={len(args) - 1: 0},
        compiler_params=_cparams(3),
        name=name,
    )(*args)


def _router_body(x_ref, g_ref, wr_ref, idx_ref, gate_ref, *, n_experts):
    xn = _rms(x_ref[...], g_ref[...])
    wr = wr_ref[...]
    xh = xn.astype(BF16)
    xl = (xn - xh.astype(F32)).astype(BF16)
    wh = wr.astype(BF16)
    wl = (wr - wh.astype(F32)).astype(BF16)
    logits = (jnp.dot(xh, wh, preferred_element_type=F32) + jnp.dot(xh, wl, preferred_element_type=F32)
              + jnp.dot(xl, wh, preferred_element_type=F32))
    lane = _iota2(logits.shape, 1)
    neg = -jnp.inf
    l1 = jnp.where(lane < n_experts, logits, neg)
    m1 = jnp.max(l1, axis=1, keepdims=True)
    i1 = jnp.min(jnp.where(l1 == m1, lane, N_EXPERTS_PAD), axis=1, keepdims=True)
    l2 = jnp.where(lane == i1, neg, l1)
    m2 = jnp.max(l2, axis=1, keepdims=True)
    i2 = jnp.min(jnp.where(l2 == m2, lane, N_EXPERTS_PAD), axis=1, keepdims=True)
    e2 = jnp.exp(m2 - m1)
    denom = 1.0 + e2
    idx_ref[...] = jnp.where(lane == 0, i1, jnp.where(lane == 1, i2, 0))
    gate_ref[...] = jnp.where(lane == 0, 1.0 / denom, jnp.where(lane == 1, e2 / denom, 0.0))


def _router(x, g, w_router):
    r, d = x.shape
    n_experts = w_router.shape[1]
    wr = jnp.pad(w_router, ((0, 0), (0, N_EXPERTS_PAD - n_experts)))
    tr = _row_tile(r, 256)
    return pl.pallas_call(
        functools.partial(_router_body, n_experts=n_experts),
        out_shape=(jax.ShapeDtypeStruct((r, N_EXPERTS_PAD), jnp.int32),
                   jax.ShapeDtypeStruct((r, N_EXPERTS_PAD), F32)),
        grid=(r // tr,),
        in_specs=[pl.BlockSpec((tr, d), lambda i: (i, 0)), pl.BlockSpec((1, d), lambda i: (0, 0)),
                  pl.BlockSpec((d, N_EXPERTS_PAD), lambda i: (0, 0))],
        out_specs=(pl.BlockSpec((tr, N_EXPERTS_PAD), lambda i: (i, 0)),
                   pl.BlockSpec((tr, N_EXPERTS_PAD), lambda i: (i, 0))),
        compiler_params=_cparams(1),
        name="moe_router",
    )(x, g.reshape(1, d), wr)


def _row_copy(src_hbm, row, dst_vmem, slot, sem):
    return pltpu.make_async_copy(src_hbm.at[pl.ds(row, 1)], dst_vmem.at[pl.ds(slot, 1)], sem)


def _gather_norm_body(src_ref, x_hbm, g_ref, o_ref, buf_ref, sem, *, tg):
    base = pl.program_id(0) * tg

    def issue(r, carry):
        _row_copy(x_hbm, src_ref[base + r], buf_ref, r, sem).start()
        return carry

    lax.fori_loop(0, tg, issue, 0)

    def wait(r, carry):
        _row_copy(x_hbm, 0, buf_ref, r, sem).wait()
        return carry

    lax.fori_loop(0, tg, wait, 0)
    o_ref[...] = _rms(buf_ref[...], g_ref[...]).astype(o_ref.dtype)


def _gather_norm(x, g, src_rows, tg):
    _, d = x.shape
    p_rows = src_rows.shape[0]
    return pl.pallas_call(
        functools.partial(_gather_norm_body, tg=tg),
        out_shape=jax.ShapeDtypeStruct((p_rows, d), BF16),
        grid_spec=pltpu.PrefetchScalarGridSpec(
            num_scalar_prefetch=1,
            grid=(p_rows // tg,),
            in_specs=[pl.BlockSpec(memory_space=pl.ANY), pl.BlockSpec((1, d), lambda i, s: (0, 0))],
            out_specs=pl.BlockSpec((tg, d), lambda i, s: (i, 0)),
            scratch_shapes=[pltpu.VMEM((tg, d), F32), pltpu.SemaphoreType.DMA(())],
        ),
        compiler_params=_cparams(1),
        name="moe_gather_norm",
    )(src_rows, x, g.reshape(1, d))


def _combine_body(pos_ref, x_ref, y_hbm, g_ref, o_ref, buf_ref, sem, *, tc, top_k):
    base = pl.program_id(0) * tc

    def issue(r, carry):
        for k in range(top_k):
            _row_copy(y_hbm, pos_ref[(base + r) * top_k + k], buf_ref.at[k], r, sem).start()
        return carry

    lax.fori_loop(0, tc, issue, 0)

    def wait(r, carry):
        for k in range(top_k):
            _row_copy(y_hbm, 0, buf_ref.at[k], r, sem).wait()
        return carry

    lax.fori_loop(0, tc, wait, 0)
    y = buf_ref[0]
    for k in range(1, top_k):
        y = y + buf_ref[k]
    o_ref[...] = _rms(x_ref[...] + y, g_ref[...])


def _combine_norm(x, y_sorted, pos, g, top_k):
    r, d = x.shape
    tc = _row_tile(r, 256)
    return pl.pallas_call(
        functools.partial(_combine_body, tc=tc, top_k=top_k),
        out_shape=jax.ShapeDtypeStruct((r, d), F32),
        grid_spec=pltpu.PrefetchScalarGridSpec(
            num_scalar_prefetch=1,
            grid=(r // tc,),
            in_specs=[pl.BlockSpec((tc, d), lambda i, s: (i, 0)), pl.BlockSpec(memory_space=pl.ANY),
                      pl.BlockSpec((1, d), lambda i, s: (0, 0))],
            out_specs=pl.BlockSpec((tc, d), lambda i, s: (i, 0)),
            scratch_shapes=[pltpu.VMEM((top_k, tc, d), F32), pltpu.SemaphoreType.DMA(())],
        ),
        compiler_params=_cparams(1),
        name="moe_combine_norm",
    )(pos, x, y_sorted, g.reshape(1, d))


MOE_ROW_TILE_MAX = 512


def _moe_row_tile(n_assign, n_experts):
    tm = 2 * SUBLANE
    while tm * 2 <= min(MOE_ROW_TILE_MAX, n_assign // (2 * n_experts)):
        tm *= 2
    return tm


def _moe_plan(top_idx, top_gate, n_experts, tm):
    r, top_k = top_idx.shape
    n_assign = r * top_k
    n_tiles = (n_assign + n_experts * (tm - 1)) // tm
    flat_e = top_idx.reshape(-1)
    onehot = (flat_e[:, None] == jnp.arange(n_experts, dtype=jnp.int32)[None, :]).astype(jnp.int32)
    csum = jnp.cumsum(onehot, axis=0)
    counts = csum[-1]
    tiles_per = (counts + tm - 1) // tm
    tile_end = jnp.cumsum(tiles_per)
    tile_start = tile_end - tiles_per
    count_start = jnp.cumsum(counts) - counts
    rank = jnp.sum((csum - onehot) * onehot, axis=1)
    pos = jnp.sum(onehot * (tile_start * tm)[None, :], axis=1) + rank
    total = tile_end[-1]
    t = jnp.arange(n_tiles, dtype=jnp.int32)
    valid = t < total
    t_eff = jnp.minimum(t, total - 1)
    te = jnp.sum((t_eff[:, None] >= tile_end[None, :]).astype(jnp.int32), axis=1)
    te = jnp.minimum(te, n_experts - 1)
    first = valid & (t == tile_start[te])
    tiles = (te, first.astype(jnp.int32), valid.astype(jnp.int32), t)
    order = jnp.argsort(flat_e, stable=True).astype(jnp.int32)
    slot = jnp.arange(n_tiles * tm, dtype=jnp.int32)
    slot_e = jnp.repeat(te, tm)
    local = slot - jnp.repeat(tile_start[te] * tm, tm)
    filled = jnp.repeat(valid, tm) & (local < counts[slot_e])
    assign = order[jnp.clip(count_start[slot_e] + local, 0, n_assign - 1)]
    src_rows = jnp.where(filled, assign // top_k, 0)
    row_gate = jnp.where(filled, top_gate.reshape(-1)[assign], 0.0)
    return tiles, pos, src_rows, row_gate.reshape(-1, 1)


def _prev_rows(rows3):
    return jnp.pad(rows3, ((0, 0), (SUBLANE - (CONV_W - 1), 0), (0, 0)))


def _last_rows(arr, col0, c, row0, nb, seq):
    blk = lax.slice(arr, (row0, col0), (row0 + nb * seq, col0 + c)).reshape(nb, seq, c)
    return blk[:, seq - (CONV_W - 1):, :]


def kernel(x_prompt, x_sample, state_ssd_conv, state_ssd, state_mlstm_conv, state_mlstm_c, state_mlstm_n,
           state_mlstm_m, state_gla, meta_tokens,
           norm_mix_even, w_in_even, ssd_conv_w, ssd_conv_b, ssd_dt_bias, ssd_a_log, ssd_d, ssd_norm,
           mlstm_conv_w, mlstm_conv_b, mlstm_if_bias, mlstm_norm, w_out_even,
           norm_ffn_even, ffn_w_gate, ffn_w_up, ffn_w_down,
           norm_mix_odd, w_in_odd, gla_w_gate_up, gla_b_gate, gla_norm, w_out_odd,
           norm_ffn_odd, moe_router, moe_w1, moe_w3, moe_w2, final_norm):
    bp, sp, d = x_prompt.shape
    bs, ss, _ = x_sample.shape
    n_meta = meta_tokens.shape[0]
    assert w_in_even.shape[0] == 1 and w_in_odd.shape[0] == 1, "one even and one odd layer"

    ssd_heads = ssd_a_log.shape[-1]
    d_inner = ssd_norm.shape[-1]
    p_dim = d_inner // ssd_heads
    conv_dim = ssd_conv_w.shape[-1]
    n_state = (conv_dim - d_inner) // (2 * SSD_GROUPS)
    hpg = ssd_heads // SSD_GROUPS
    ml_heads = mlstm_if_bias.shape[-1] // 2
    ml_v = mlstm_norm.shape[-1]
    ml_dv = ml_v // ml_heads
    ml_qk = mlstm_conv_w.shape[-1] // 2
    ml_dk = ml_qk // ml_heads
    gla_heads, gla_dk, gla_dv = state_gla.shape[2:]
    gla_rank = gla_w_gate_up.shape[1]
    n_experts = moe_router.shape[-1]
    top_k = 2

    n_p, n_s, n_m = bp * sp, bs * ss, bp * n_meta
    rows = n_p + n_s + n_m
    row_p, row_s, row_m = 0, n_p, n_p + n_s
    seg_m = (row_m, bp, n_meta, (n_meta,) * 3)
    seg_p = (row_p, bp, sp, PROMPT_CHUNKS)
    seg_s = (row_s, bs, ss, (ss,) * 3)

    meta_rows = jnp.broadcast_to(meta_tokens.astype(F32)[None], (bp, n_meta, d)).reshape(n_m, d)
    x0 = jnp.concatenate([x_prompt.reshape(n_p, d), x_sample.reshape(n_s, d), meta_rows], axis=0)

    xn = _rmsnorm(x0, norm_mix_even[0])
    w_in = w_in_even
    off_z, off_xbc = 0, d_inner
    off_dt = off_xbc + conv_dim
    off_qk = off_dt + ssd_heads
    off_v = off_qk + 2 * ml_qk
    off_if = off_v + ml_v
    off_o = off_if + 2 * ml_heads
    n_a = off_dt
    xw_a = _dense_mm(xn, w_in, n_cols=n_a, col0=0, name="in_even_a")
    w_b = jnp.concatenate([w_in[:, :, off_qk:off_if], w_in[:, :, off_o:off_o + ml_v]], axis=2)
    xw_b = _dense_mm(xn, w_b, name="in_even_b")
    n_small = ssd_heads + 2 * ml_heads
    w_s = jnp.concatenate([w_in[:, :, off_dt:off_qk], w_in[:, :, off_if:off_o]], axis=2)
    w_s = jnp.pad(w_s, ((0, 0), (0, 0), (0, LANE - n_small)))
    small = _dense_mm(xn, w_s, name="in_even_small")

    dt_g = small[:, :ssd_heads].reshape(rows, SSD_GROUPS, hpg).transpose(1, 0, 2)
    par_g = jnp.stack([ssd_dt_bias[0], ssd_a_log[0], ssd_d[0]], axis=0).reshape(3, SSD_GROUPS, hpg)
    par_g = par_g.transpose(1, 0, 2)
    gt_h = small[:, ssd_heads:n_small].reshape(rows, 2, ml_heads).transpose(2, 0, 1)
    gb_h = mlstm_if_bias[0].reshape(2, ml_heads).transpose(1, 0).reshape(ml_heads, 1, 2)
    ssd_gn = ssd_norm[0].reshape(1, d_inner)
    ml_gn = mlstm_norm[0].reshape(1, ml_v)

    def even_segment(seg, prev_ssd, prev_ml, st, mix):
        row0, nb, seq, qs = seg
        xbc_c = _conv_silu(xw_a, off_xbc, conv_dim, row0, nb, seq, _prev_rows(prev_ssd),
                           ssd_conv_w[0], ssd_conv_b[0], "ssd_conv")
        qk_c = _conv_silu(xw_b, 0, 2 * ml_qk, row0, nb, seq, _prev_rows(prev_ml),
                          mlstm_conv_w[0], mlstm_conv_b[0], "mlstm_conv")
        h0 = None if st is None else st[0]
        mix, h_new = _ssd(xbc_c, xw_a, off_z, dt_g, par_g, ssd_gn, h0, mix, 0, row0, nb, seq,
                          math.gcd(seq, qs[0]), hpg, p_dim, n_state, SSD_BLOCKING, "ssd_scan")
        ml0 = None if st is None else st[1:]
        mix, c_new, n_new, m_new = _mlstm(qk_c, xw_b, 2 * ml_qk, 2 * ml_qk + ml_v, gt_h, gb_h, ml_gn, ml0,
                                          mix, d_inner, row0, nb, seq, math.gcd(seq, qs[1]), ml_heads, ml_dk,
                                          ml_dv, MLSTM_BLOCKING, "mlstm_scan")
        new_conv_ssd = _last_rows(xw_a, off_xbc, conv_dim, row0, nb, seq)
        new_conv_ml = _last_rows(xw_b, 0, 2 * ml_qk, row0, nb, seq)
        return mix, (h_new, c_new, n_new, m_new), new_conv_ssd, new_conv_ml

    zeros_ssd = jnp.zeros((bp, CONV_W - 1, conv_dim), F32)
    zeros_ml = jnp.zeros((bp, CONV_W - 1, 2 * ml_qk), F32)
    mix = jnp.zeros((rows, d_inner + ml_v), BF16)
    mix, st_m, cs_m, cm_m = even_segment(seg_m, zeros_ssd, zeros_ml, None, mix)
    mix, st_p, cs_p, cm_p = even_segment(seg_p, cs_m, cm_m, st_m, mix)
    st_s0 = (state_ssd[0].reshape(bs, SSD_GROUPS, hpg * p_dim, n_state), state_mlstm_c[0],
             state_mlstm_n[0].reshape(bs, ml_heads, 1, ml_dk), state_mlstm_m[0].reshape(bs, ml_heads, 1, 1))
    mix, st_s, cs_s, cm_s = even_segment(seg_s, state_ssd_conv[0], state_mlstm_conv[0], st_s0, mix)

    x1 = _dense_mm(mix, w_out_even, res=x0, name="out_even")
    xn = _rmsnorm(x1, norm_ffn_even[0])
    hid = _dense_swiglu(xn, ffn_w_gate, ffn_w_up, "ffn_up")
    x2 = _dense_mm(hid, ffn_w_down, res=x1, name="ffn_down")

    xn = _rmsnorm(x2, norm_mix_odd[0])
    n_main = 2 * gla_heads * gla_dk + 2 * gla_heads * gla_dv
    xw_o = _dense_mm(xn, w_in_odd, n_cols=n_main, col0=0, name="in_odd")
    w_ga = jnp.pad(w_in_odd[:, :, n_main:n_main + gla_rank], ((0, 0), (0, 0), (0, LANE - gla_rank)))
    small_o = _dense_mm(xn, w_ga, name="in_odd_small")
    gla_gn = gla_norm[0].reshape(1, gla_heads * gla_dv)
    bg = gla_b_gate[0].reshape(1, gla_heads * gla_dk)
    wgu_pad = jnp.pad(gla_w_gate_up[0], ((0, LANE - gla_rank), (0, 0)))
    k_col0 = gla_heads * gla_dk
    v_col0 = 2 * gla_heads * gla_dk
    r_col0 = v_col0 + gla_heads * gla_dv

    def odd_segment(seg, s0, mix):
        row0, nb, seq, qs = seg
        return _gla(xw_o, 0, k_col0, v_col0, r_col0, small_o, wgu_pad, bg, gla_gn, s0, mix, row0, nb, seq,
                    math.gcd(seq, qs[2]), gla_heads, gla_dk, gla_dv, GLA_BLOCKING, "gla_scan")

    mix = jnp.zeros((rows, gla_heads * gla_dv), BF16)
    mix, s_m = odd_segment(seg_m, None, mix)
    mix, s_p = odd_segment(seg_p, s_m, mix)
    mix, s_s = odd_segment(seg_s, state_gla[0], mix)
    x3 = _dense_mm(mix, w_out_odd, res=x2, name="out_odd")

    idx_pad, gate_pad = _router(x3, norm_ffn_odd[0], moe_router[0])
    tm_e = _moe_row_tile(rows * top_k, n_experts)
    tiles_e, pos, src_rows, row_gate = _moe_plan(idx_pad[:, :top_k], gate_pad[:, :top_k], n_experts, tm_e)
    xs = _gather_norm(x3, norm_ffn_odd[0], src_rows, _pick(tm_e, (256, 128, 64, 32, 16, 8)))
    _, tn_up = _plan_swiglu(d, moe_w1.shape[-1], (tm_e,))
    hs = _gmm_swiglu(xs, moe_w1[0], moe_w3[0], tiles_e, tm=tm_e, tn=tn_up, name="moe_up")
    _, tn_dn, tk_dn = _plan_mm(moe_w2.shape[-2], d, 0, (tm_e,), False)
    ys = _gmm(hs, moe_w2[0], tiles_e, tm=tm_e, tn=tn_dn, tk=tk_dn, scale=row_gate, name="moe_down")
    y = _combine_norm(x3, ys, pos, final_norm, top_k)

    y_prompt = y[row_p:row_p + n_p].reshape(bp, sp, d)
    y_sample = y[row_s:row_s + n_s].reshape(bs, ss, d)

    def pack_even(cs, cm, st, nb):
        h_new, c_new, n_new, m_new = st
        return (cs[None], h_new.reshape(1, nb, ssd_heads, p_dim, n_state), cm[None], c_new[None],
                n_new.reshape(1, nb, ml_heads, ml_dk), m_new.reshape(1, nb, ml_heads))

    return ((y_prompt, y_sample) + pack_even(cs_p, cm_p, st_p, bp) + (s_p[None],)
            + pack_even(cs_s, cm_s, st_s, bs) + (s_s[None],))
```

```python
import functools
import math

import jax
import jax.numpy as jnp
from jax import lax
from jax.experimental import pallas as pl
from jax.experimental.pallas import tpu as pltpu

F32 = jnp.float32
BF16 = jnp.bfloat16

EPS = 1e-6
CONV_W = 4
SSD_GROUPS = 8
GLA_TAU = 16.0
N_EXPERTS_PAD = 128
PROMPT_CHUNKS = (128, 256, 64)
SSD_BLOCKING = (4, 2)
MLSTM_BLOCKING = (4, 2)
GLA_BLOCKING = (1, 2)

V7X_VMEM_BYTES = 64 * 1024 * 1024
VMEM_LIMIT = V7X_VMEM_BYTES - 8 * 1024 * 1024
VMEM_BUDGET = VMEM_LIMIT - 4 * 1024 * 1024
LANE = 128
SUBLANE = 8
TK_MAX = 4096


def _cparams(n_axes):
    return pltpu.CompilerParams(dimension_semantics=("arbitrary",) * n_axes, vmem_limit_bytes=VMEM_LIMIT)


def _pick(n, candidates):
    for c in candidates:
        if c <= n and n % c == 0:
            return c
    return n


def _row_tile(n, cap):
    if n <= cap:
        return n
    best = SUBLANE
    for c in range(SUBLANE, cap + 1, SUBLANE):
        if n % c == 0:
            best = c
    return best


def _dot(a, b, dims=((1,), (0,))):
    return lax.dot_general(a.astype(BF16), b.astype(BF16), (dims, ((), ())), preferred_element_type=F32)


def _split3(x):
    hi = x.astype(BF16)
    r1 = x - hi.astype(F32)
    mid = r1.astype(BF16)
    lo = (r1 - mid.astype(F32)).astype(BF16)
    return hi, mid, lo


def _dot_exact_rhs(a01, b, dims=((1,), (0,))):
    a = a01.astype(BF16)
    out = None
    for part in _split3(b):
        t = lax.dot_general(a, part, (dims, ((), ())), preferred_element_type=F32)
        out = t if out is None else out + t
    return out


def _dot_exact_lhs(a, b01, dims=((1,), (0,))):
    b = b01.astype(BF16)
    out = None
    for part in _split3(a):
        t = lax.dot_general(part, b, (dims, ((), ())), preferred_element_type=F32)
        out = t if out is None else out + t
    return out


def _iota2(shape, axis):
    return lax.broadcasted_iota(jnp.int32, shape, axis)


def _tril(q):
    return _iota2((q, q), 0) >= _iota2((q, q), 1)


def _transpose_small(x, q):
    eye = (_iota2((q, q), 0) == _iota2((q, q), 1)).astype(F32)
    return _dot_exact_lhs(x, eye, dims=((0,), (0,)))


def _silu(x):
    return x * jax.nn.sigmoid(x)


def _rms(x, g):
    return x * lax.rsqrt(jnp.mean(x * x, axis=-1, keepdims=True) + EPS) * g


def _rmsnorm_body(x_ref, g_ref, o_ref):
    o_ref[...] = _rms(x_ref[...], g_ref[...]).astype(o_ref.dtype)


def _rmsnorm(x, g, out_dtype=BF16):
    r, d = x.shape
    tr = _row_tile(r, 256)
    return pl.pallas_call(
        _rmsnorm_body,
        out_shape=jax.ShapeDtypeStruct((r, d), out_dtype),
        grid=(r // tr,),
        in_specs=[pl.BlockSpec((tr, d), lambda i: (i, 0)), pl.BlockSpec((1, d), lambda i: (0, 0))],
        out_specs=pl.BlockSpec((tr, d), lambda i: (i, 0)),
        compiler_params=_cparams(1),
        name="rmsnorm",
    )(x, g.reshape(1, d))


def _tile_rows_dispatch(nrows, tm, sub, rows_block, zero_block):
    @pl.when(nrows == tm)
    def _full():
        rows_block(0, tm)

    if sub is None:
        @pl.when(nrows == 0)
        def _empty():
            zero_block(0, tm)
        return
    for s in range(tm // sub):
        @pl.when((nrows < tm) & (nrows > s * sub))
        def _piece(s=s):
            rows_block(s * sub, sub)

        @pl.when(nrows <= s * sub)
        def _zero(s=s):
            zero_block(s * sub, sub)


def _gmm_body(te_ref, tf_ref, tn_ref, tr_ref, x_ref, w_ref, *rest, kc, tm, sub, has_res, has_scale):
    del te_ref, tr_ref
    rest = list(rest)
    res_ref = rest.pop(0) if has_res else None
    scale_ref = rest.pop(0) if has_scale else None
    o_ref, wbf_ref = rest[0], rest[1]
    acc_ref = rest[2] if kc > 1 else None
    t = pl.program_id(1)
    c = pl.program_id(2)

    @pl.when(tf_ref[t] == 1)
    def _cast():
        wbf_ref[c] = w_ref[...].astype(BF16)

    def rows_block(r0, n):
        rs = slice(r0, r0 + n)

        def finish(v):
            if has_scale:
                v = v * scale_ref[rs, :]
            if has_res:
                v = v + res_ref[rs, :]
            o_ref[rs, :] = v.astype(o_ref.dtype)

        part = jnp.dot(x_ref[rs, :], wbf_ref[c], preferred_element_type=F32)
        if kc == 1:
            finish(part)
        else:
            @pl.when(c == 0)
            def _():
                acc_ref[rs, :] = part

            @pl.when(c > 0)
            def _():
                acc_ref[rs, :] += part

            @pl.when(c == kc - 1)
            def _():
                finish(acc_ref[rs, :])

    def zero_block(r0, n):
        o_ref[r0:r0 + n, :] = jnp.zeros((n, o_ref.shape[1]), o_ref.dtype)

    _tile_rows_dispatch(tn_ref[t], tm, sub, rows_block, zero_block)


def _dense_tiles(n_tiles, tm):
    t = jnp.arange(n_tiles, dtype=jnp.int32)
    return (jnp.zeros((n_tiles,), jnp.int32), (t == 0).astype(jnp.int32), jnp.full((n_tiles,), tm, jnp.int32), t)


def _tk_options(k):
    if k <= TK_MAX:
        return [k]
    divs = [c for c in range(LANE, k, LANE) if k % c == 0]
    good = sorted((c for c in divs if TK_MAX // 2 <= c <= TK_MAX), reverse=True)
    over = sorted(c for c in divs if TK_MAX < c <= 2 * TK_MAX)
    return good + over or sorted(divs, reverse=True)


def _tm_options(rows):
    opts = [c for c in range(2 * SUBLANE, min(rows, 1024) + 1, 2 * SUBLANE) if rows % c == 0]
    return sorted(opts, reverse=True) or [rows]


def _plan_mm(k, n_cols, col0, tm_options, has_res):
    for tn in (512, 256, 128):
        if n_cols % tn or col0 % tn:
            continue
        for tm in tm_options:
            for tk in _tk_options(k):
                kc = k // tk
                need = (2 * tk * tn * 4 + k * tn * 2 + 2 * tm * tk * 2 + 3 * tm * tn * 4
                        + (2 * tm * tn * 4 if has_res else 0) + (tm * tn * 4 if kc > 1 else 0))
                if need <= VMEM_BUDGET:
                    return tm, tn, tk
    raise ValueError(f"no matmul tiling fits VMEM for k={k} n={n_cols}")


def _plan_swiglu(k, f, tm_options):
    for tn in (512, 256, 128):
        if f % tn:
            continue
        for tm in tm_options:
            need = 2 * (2 * k * tn * 4) + 2 * k * tn * 2 + 2 * tm * k * 2 + 2 * tm * tn * 2 + 3 * tm * tn * 4
            if need <= VMEM_BUDGET:
                return tm, tn
    raise ValueError(f"no swiglu tiling fits VMEM for k={k} f={f}")


def _gmm(x, w, tiles, *, tm, tn, tk, sub=None, n_cols=None, col0=0, res=None, scale=None, out_dtype=F32,
         name="gmm"):
    te, tf, tv, tr = tiles
    n_tiles = te.shape[0]
    rows, k = x.shape
    n_total = w.shape[2]
    n_cols = n_total if n_cols is None else n_cols
    assert col0 % tn == 0 and n_cols % tn == 0 and rows % tm == 0 and k % tk == 0
    assert sub is None or tm % sub == 0
    kc = k // tk
    j0 = col0 // tn

    def x_map(j, t, c, te, tf, tv, tr):
        return (tr[t], jnp.where(tv[t] > 0, c, kc - 1))

    def w_map(j, t, c, te, tf, tv, tr):
        return (te[t], jnp.where(tf[t] == 1, c, kc - 1), j0 + j)

    def o_map(j, t, c, te, tf, tv, tr):
        return (tr[t], j)

    in_specs = [pl.BlockSpec((tm, tk), x_map), pl.BlockSpec((None, tk, tn), w_map)]
    args = [x, w]
    if res is not None:
        in_specs.append(pl.BlockSpec((tm, tn), o_map))
        args.append(res)
    if scale is not None:
        in_specs.append(pl.BlockSpec((tm, 1), lambda j, t, c, te, tf, tv, tr: (tr[t], 0)))
        args.append(scale)
    scratch = [pltpu.VMEM((kc, tk, tn), BF16)]
    if kc > 1:
        scratch.append(pltpu.VMEM((tm, tn), F32))
    return pl.pallas_call(
        functools.partial(_gmm_body, kc=kc, tm=tm, sub=sub, has_res=res is not None,
                          has_scale=scale is not None),
        out_shape=jax.ShapeDtypeStruct((rows, n_cols), out_dtype),
        grid_spec=pltpu.PrefetchScalarGridSpec(
            num_scalar_prefetch=4,
            grid=(n_cols // tn, n_tiles, kc),
            in_specs=in_specs,
            out_specs=pl.BlockSpec((tm, tn), o_map),
            scratch_shapes=scratch,
        ),
        compiler_params=_cparams(3),
        name=name,
    )(te, tf, tv, tr, *args)


def _gmm_swiglu_body(te_ref, tf_ref, tn_ref, tr_ref, x_ref, wg_ref, wu_ref, o_ref, wgb_ref, wub_ref, *, tm, sub):
    del te_ref, tr_ref
    t = pl.program_id(1)

    @pl.when(tf_ref[t] == 1)
    def _cast():
        wgb_ref[...] = wg_ref[...].astype(BF16)
        wub_ref[...] = wu_ref[...].astype(BF16)

    def rows_block(r0, n):
        x = x_ref[r0:r0 + n, :]
        g = jnp.dot(x, wgb_ref[...], preferred_element_type=F32)
        u = jnp.dot(x, wub_ref[...], preferred_element_type=F32)
        o_ref[r0:r0 + n, :] = (_silu(g) * u).astype(o_ref.dtype)

    def zero_block(r0, n):
        o_ref[r0:r0 + n, :] = jnp.zeros((n, o_ref.shape[1]), o_ref.dtype)

    _tile_rows_dispatch(tn_ref[t], tm, sub, rows_block, zero_block)


def _gmm_swiglu(x, wg, wu, tiles, *, tm, tn, sub=None, name="gmm_swiglu"):
    te, tf, tv, tr = tiles
    n_tiles = te.shape[0]
    rows, k = x.shape
    f = wg.shape[2]
    assert rows % tm == 0 and f % tn == 0

    def x_map(j, t, te, tf, tv, tr):
        return (tr[t], 0)

    def w_map(j, t, te, tf, tv, tr):
        return (te[t], 0, j)

    def o_map(j, t, te, tf, tv, tr):
        return (tr[t], j)

    return pl.pallas_call(
        functools.partial(_gmm_swiglu_body, tm=tm, sub=sub),
        out_shape=jax.ShapeDtypeStruct((rows, f), BF16),
        grid_spec=pltpu.PrefetchScalarGridSpec(
            num_scalar_prefetch=4,
            grid=(f // tn, n_tiles),
            in_specs=[pl.BlockSpec((tm, k), x_map), pl.BlockSpec((None, k, tn), w_map),
                      pl.BlockSpec((None, k, tn), w_map)],
            out_specs=pl.BlockSpec((tm, tn), o_map),
            scratch_shapes=[pltpu.VMEM((k, tn), BF16), pltpu.VMEM((k, tn), BF16)],
        ),
        compiler_params=_cparams(2),
        name=name,
    )(te, tf, tv, tr, x, wg, wu)


def _dense_mm(x, w, *, n_cols=None, col0=0, res=None, name):
    rows, k = x.shape
    n = w.shape[2] if n_cols is None else n_cols
    tm, tn, tk = _plan_mm(k, n, col0, _tm_options(rows), res is not None)
    return _gmm(x, w, _dense_tiles(rows // tm, tm), tm=tm, tn=tn, tk=tk, n_cols=n_cols, col0=col0, res=res,
                name=name)


def _mm_nt_body(off_ref, x_ref, w_ref, o_ref, wbf_ref):
    del off_ref

    @pl.when(pl.program_id(1) == 0)
    def _cast():
        wbf_ref[...] = w_ref[...].astype(BF16)

    o_ref[...] = lax.dot_general(x_ref[...], wbf_ref[...], (((1,), (1,)), ((), ())), preferred_element_type=F32)


def _dense_mm_nt(x, wt, row_offsets, tn, name):
    rows, k = x.shape
    nj = len(row_offsets)
    assert all(o % SUBLANE == 0 and o + tn <= wt.shape[0] for o in row_offsets)
    tm = next(t for t in _tm_options(rows)
              if 2 * tn * k * 4 + tn * k * 2 + 2 * t * k * 2 + 3 * t * tn * 4 <= VMEM_BUDGET)
    return pl.pallas_call(
        _mm_nt_body,
        out_shape=jax.ShapeDtypeStruct((rows, nj * tn), F32),
        grid_spec=pltpu.PrefetchScalarGridSpec(
            num_scalar_prefetch=1,
            grid=(nj, rows // tm),
            in_specs=[pl.BlockSpec((tm, k), lambda j, i, off: (i, 0)),
                      pl.BlockSpec((pl.Element(tn), pl.Element(k)),
                                   lambda j, i, off: (pl.multiple_of(off[j] * SUBLANE, SUBLANE), 0))],
            out_specs=pl.BlockSpec((tm, tn), lambda j, i, off: (i, j)),
            scratch_shapes=[pltpu.VMEM((tn, k), BF16)],
        ),
        compiler_params=_cparams(2),
        name=name,
    )(jnp.asarray([o // SUBLANE for o in row_offsets], jnp.int32), x, wt)


def _dense_swiglu(x, wg, wu, name):
    rows, k = x.shape
    tm, tn = _plan_swiglu(k, wg.shape[2], _tm_options(rows))
    return _gmm_swiglu(x, wg, wu, _dense_tiles(rows // tm, tm), tm=tm, tn=tn, name=name)


def _conv_body(prev_ref, u_ref, w_ref, b_ref, o_ref, tail_ref, full_ref, *, seq, bb):
    for bi in range(bb):
        r0 = bi * seq
        f0 = bi * (SUBLANE + seq)
        full_ref[f0:f0 + SUBLANE, :] = prev_ref[bi]
        full_ref[f0 + SUBLANE:f0 + SUBLANE + seq, :] = u_ref[r0:r0 + seq, :]
        tail_ref[bi] = full_ref[f0 + seq:f0 + seq + SUBLANE, :]
        acc = b_ref[...]
        for k in range(CONV_W):
            start = f0 + SUBLANE - (CONV_W - 1) + k
            acc = acc + full_ref[start:start + seq, :] * w_ref[k:k + 1, :]
        o_ref[r0:r0 + seq, :] = _silu(acc)


CONV_BLOCK_BYTES = 2 * 1024 * 1024


def _conv_silu(u_arr, col0, c, row0, nb, seq, prev, w, b, name):
    ct = LANE
    for cand in (2048, 1024, 512, 256):
        if c % cand == 0 and col0 % cand == 0 and seq * cand * 4 <= CONV_BLOCK_BYTES:
            ct = cand
            break
    bb = 1
    while bb * 2 <= 16 and nb % (bb * 2) == 0 and bb * 2 * seq * ct * 4 <= CONV_BLOCK_BYTES:
        bb *= 2
    assert col0 % ct == 0 and row0 % (bb * seq) == 0
    rb0, cb0 = row0 // (bb * seq), col0 // ct
    return pl.pallas_call(
        functools.partial(_conv_body, seq=seq, bb=bb),
        out_shape=(jax.ShapeDtypeStruct((nb * seq, c), F32), jax.ShapeDtypeStruct((nb, SUBLANE, c), F32)),
        grid=(nb // bb, c // ct),
        in_specs=[pl.BlockSpec((bb, SUBLANE, ct), lambda b_, j: (b_, 0, j)),
                  pl.BlockSpec((bb * seq, ct), lambda b_, j: (rb0 + b_, cb0 + j)),
                  pl.BlockSpec((CONV_W, ct), lambda b_, j: (0, j)),
                  pl.BlockSpec((1, ct), lambda b_, j: (0, j))],
        out_specs=(pl.BlockSpec((bb * seq, ct), lambda b_, j: (b_, j)),
                   pl.BlockSpec((bb, SUBLANE, ct), lambda b_, j: (b_, 0, j))),
        scratch_shapes=[pltpu.VMEM((bb * (SUBLANE + seq), ct), F32)],
        compiler_params=_cparams(2),
        name=name,
    )(prev, u_arr, w, b.reshape(1, c))


def _scan_blocking(nb, units, nc, bb_max, ub_max, col_offsets=(), unit_width=1):
    bb = 1
    if nc == 1:
        while bb * 2 <= bb_max and nb % (bb * 2) == 0:
            bb *= 2
    ub = 1
    while (ub * 2 <= ub_max and units % (ub * 2) == 0
           and all(c % (ub * 2 * unit_width) == 0 for c in col_offsets)):
        ub *= 2
    return bb, ub


def _init_carried(carry, zero_init, pairs):
    if not carry:
        return

    @pl.when(pl.program_id(2) == 0)
    def _init():
        for out_ref, in_ref in pairs:
            out_ref[...] = jnp.zeros_like(out_ref) if zero_init else in_ref[...]


def _read_state(carry, zero_init, out_ref, in_ref, bi, ui):
    if carry:
        return out_ref[bi, ui]
    if zero_init:
        return jnp.zeros(out_ref.shape[2:], out_ref.dtype)
    return in_ref[bi, ui]


def _ssd_body(*refs, q, hpg, p, bb, gb, zero_init, carry):
    refs = list(refs)
    x_ref, b_ref, c_ref, z_ref, dt_ref, par_ref, gn_ref = refs[:7]
    pos = 7
    h0_ref = None
    if not zero_init:
        h0_ref = refs[pos]
        pos += 1
    y_ref, h_ref, ys_ref = refs[pos + 1:pos + 4]
    _init_carried(carry, zero_init, [(h_ref, h0_ref)])

    gw = hpg * p
    n = b_ref.shape[1] // gb
    tril = _tril(q)
    trilf = tril.astype(F32)
    expand = (_iota2((hpg, gw), 1) // p == _iota2((hpg, gw), 0)).astype(F32)
    expand_t = (_iota2((gw, hpg), 0) // p == _iota2((gw, hpg), 1)).astype(F32)
    for bi in range(bb):
        r0 = bi * q
        for gi in range(gb):
            c0 = gi * gw
            x = x_ref[r0:r0 + q, c0:c0 + gw]
            bm = b_ref[r0:r0 + q, gi * n:(gi + 1) * n]
            cm = c_ref[r0:r0 + q, gi * n:(gi + 1) * n]
            par = par_ref[gi]
            dt = jax.nn.softplus(dt_ref[gi, r0:r0 + q, :] + par[0:1, :])
            a = -jnp.exp(par[1:2, :])
            dta = dt * a
            cum = _dot_exact_rhs(trilf, dta)
            cum_t = _transpose_small(cum, q)
            dt_t = _transpose_small(dt, q)
            hstate = _read_state(carry, zero_init, h_ref, h0_ref, bi, gi)
            cb = _dot(cm, bm, dims=((1,), (1,)))
            y_inter = _dot(cm, hstate, dims=((1,), (1,)))
            cum_last = cum[q - 1:q, :]
            w_end = jnp.exp(cum_last - cum) * dt
            ecum_w = _dot_exact_lhs(jnp.exp(cum), expand)
            wend_w = _dot_exact_lhs(w_end, expand)
            d_w = _dot_exact_lhs(par[2:3, :], expand)
            for h in range(hpg):
                seg = cum[:, h:h + 1] - cum_t[h:h + 1, :]
                lmat = jnp.exp(jnp.where(tril, seg, -jnp.inf))
                scores = cb * lmat * dt_t[h:h + 1, :]
                ys_ref[r0:r0 + q, c0 + h * p:c0 + (h + 1) * p] = _dot(scores, x[:, h * p:(h + 1) * p])
            y = ys_ref[r0:r0 + q, c0:c0 + gw] + y_inter * ecum_w + d_w * x
            y = y * _silu(z_ref[r0:r0 + q, c0:c0 + gw])
            y_ref[r0:r0 + q, c0:c0 + gw] = _rms(y, gn_ref[:, c0:c0 + gw]).astype(y_ref.dtype)
            last_col = jnp.broadcast_to(cum_t[:, q - 1:q], (hpg, LANE))
            decay = jnp.exp(_dot_exact_rhs(expand_t, last_col))
            if n != LANE:
                decay = jnp.broadcast_to(decay[:, 0:1], hstate.shape)
            h_ref[bi, gi] = decay * hstate + _dot(x * wend_w, bm, dims=((0,), (0,)))


def _ssd(xbc_c, xw_a, z_col0, dt_g, par_g, gnorm, h0, mix, y_col0, row0, nb, seq, q, hpg, p, n, blocking, name):
    g_cnt = SSD_GROUPS
    nc = seq // q
    gw = hpg * p
    d_inner = g_cnt * gw
    bb, gb = _scan_blocking(nb, g_cnt, nc, *blocking)
    rq, cw = bb * q, gb * gw
    assert row0 % rq == 0 and z_col0 % cw == 0 and y_col0 % cw == 0 and d_inner % (gb * n) == 0
    rb0 = row0 // rq
    zb0, yb0 = z_col0 // cw, y_col0 // cw
    bb0 = d_inner // (gb * n)
    cb0 = (d_inner + g_cnt * n) // (gb * n)
    zero_init = h0 is None

    in_specs = [
        pl.BlockSpec((rq, cw), lambda b, g, c: (b * nc + c, g)),
        pl.BlockSpec((rq, gb * n), lambda b, g, c: (b * nc + c, bb0 + g)),
        pl.BlockSpec((rq, gb * n), lambda b, g, c: (b * nc + c, cb0 + g)),
        pl.BlockSpec((rq, cw), lambda b, g, c: (rb0 + b * nc + c, zb0 + g)),
        pl.BlockSpec((gb, rq, hpg), lambda b, g, c: (g, rb0 + b * nc + c, 0)),
        pl.BlockSpec((gb, 3, hpg), lambda b, g, c: (g, 0, 0)),
        pl.BlockSpec((1, cw), lambda b, g, c: (0, g)),
    ]
    args = [xbc_c, xbc_c, xbc_c, xw_a, dt_g, par_g, gnorm]
    st_spec = pl.BlockSpec((bb, gb, gw, n), lambda b, g, c: (b, g, 0, 0))
    if not zero_init:
        in_specs.append(st_spec)
        args.append(h0)
    in_specs.append(pl.BlockSpec(memory_space=pl.ANY))
    args.append(mix)
    return pl.pallas_call(
        functools.partial(_ssd_body, q=q, hpg=hpg, p=p, bb=bb, gb=gb, zero_init=zero_init, carry=nc > 1),
        out_shape=(jax.ShapeDtypeStruct(mix.shape, mix.dtype),
                   jax.ShapeDtypeStruct((nb, g_cnt, gw, n), F32)),
        grid=(nb // bb, g_cnt // gb, nc),
        in_specs=in_specs,
        out_specs=(pl.BlockSpec((rq, cw), lambda b, g, c: (rb0 + b * nc + c, yb0 + g)), st_spec),
        scratch_shapes=[pltpu.VMEM((rq, cw), F32)],
        input_output_aliases={len(args) - 1: 0},
        compiler_params=_cparams(3),
        name=name,
    )(*args)


def _mlstm_body(*refs, q, dk, dv, bb, hb, zero_init, carry):
    refs = list(refs)
    q_ref, k_ref, v_ref, o_ref, gt_ref, gb_ref, gn_ref = refs[:7]
    pos = 7
    c0_ref = n0_ref = m0_ref = None
    if not zero_init:
        c0_ref, n0_ref, m0_ref = refs[pos:pos + 3]
        pos += 3
    y_ref, c_ref, n_ref, m_ref = refs[pos + 1:pos + 5]
    _init_carried(carry, zero_init, [(c_ref, c0_ref), (n_ref, n0_ref), (m_ref, m0_ref)])

    tril = _tril(q)
    trilf = tril.astype(F32)
    lane0 = _iota2((q, LANE), 1) == 0
    for bi in range(bb):
        r0 = bi * q
        for hi in range(hb):
            qh = q_ref[r0:r0 + q, hi * dk:(hi + 1) * dk] * (dk ** -0.5)
            kh = k_ref[r0:r0 + q, hi * dk:(hi + 1) * dk]
            vh = v_ref[r0:r0 + q, hi * dv:(hi + 1) * dv]
            gates = gt_ref[hi, r0:r0 + q, :] + gb_ref[hi]
            log_i = gates[:, 0:1]
            log_f = jax.nn.log_sigmoid(gates[:, 1:2])
            cumf = _dot_exact_rhs(trilf, jnp.broadcast_to(log_f, (q, LANE)))[:, 0:1]
            rows = _transpose_small(jnp.where(lane0, cumf, log_i), q)
            cumf_r, logi_r = rows[0:1, :], rows[1:2, :]
            m_prev = _read_state(carry, zero_init, m_ref, m0_ref, bi, hi)
            cmat = _read_state(carry, zero_init, c_ref, c0_ref, bi, hi)
            nvec = _read_state(carry, zero_init, n_ref, n0_ref, bi, hi)
            dmat = jnp.where(tril, cumf - cumf_r + logi_r, -jnp.inf)
            from_state = cumf + m_prev
            m_row = jnp.maximum(jnp.max(dmat, axis=1, keepdims=True), from_state)
            w = jnp.exp(dmat - m_row)
            s_state = jnp.exp(from_state - m_row)
            qk = _dot(qh, kh, dims=((1,), (1,))) * w
            num = _dot(qk, vh) + _dot(qh, cmat) * s_state
            den = jnp.sum(qk, axis=1, keepdims=True) + jnp.sum(qh * nvec, axis=1, keepdims=True) * s_state
            den = jnp.maximum(jnp.abs(den), jnp.exp(-m_row))
            hout = num / den
            m_new = m_row[q - 1:q, :]
            cumf_last = cumf[q - 1:q, :]
            w_end = jnp.exp(cumf_last - cumf + log_i - m_new)
            s_end = jnp.exp(cumf_last + m_prev - m_new)
            kw = kh * w_end
            c_ref[bi, hi] = s_end * cmat + _dot(kw, vh, dims=((0,), (0,)))
            n_ref[bi, hi] = s_end * nvec + jnp.sum(kw, axis=0, keepdims=True)
            m_ref[bi, hi] = m_new
            gate_o = jax.nn.sigmoid(o_ref[r0:r0 + q, hi * dv:(hi + 1) * dv])
            y_ref[r0:r0 + q, hi * dv:(hi + 1) * dv] = (
                _rms(hout, gn_ref[:, hi * dv:(hi + 1) * dv]) * gate_o).astype(y_ref.dtype)


def _mlstm(qk_c, xw_b, v_col0, o_col0, gt_h, gb_h, gnorm, state0, mix, y_col0, row0, nb, seq, q, heads, dk, dv,
           blocking, name):
    nc = seq // q
    bb, hb = _scan_blocking(nb, heads, nc, *blocking, col_offsets=(v_col0, o_col0, y_col0), unit_width=dv)
    rq = bb * q
    assert row0 % rq == 0 and v_col0 % (hb * dv) == 0 and o_col0 % (hb * dv) == 0 and y_col0 % (hb * dv) == 0
    rb0 = row0 // rq
    vb0, ob0, yb0 = v_col0 // (hb * dv), o_col0 // (hb * dv), y_col0 // (hb * dv)
    kb0 = heads // hb
    zero_init = state0 is None
    in_specs = [
        pl.BlockSpec((rq, hb * dk), lambda b, h, c: (b * nc + c, h)),
        pl.BlockSpec((rq, hb * dk), lambda b, h, c: (b * nc + c, kb0 + h)),
        pl.BlockSpec((rq, hb * dv), lambda b, h, c: (rb0 + b * nc + c, vb0 + h)),
        pl.BlockSpec((rq, hb * dv), lambda b, h, c: (rb0 + b * nc + c, ob0 + h)),
        pl.BlockSpec((hb, rq, 2), lambda b, h, c: (h, rb0 + b * nc + c, 0)),
        pl.BlockSpec((hb, 1, 2), lambda b, h, c: (h, 0, 0)),
        pl.BlockSpec((1, hb * dv), lambda b, h, c: (0, h)),
    ]
    args = [qk_c, qk_c, xw_b, xw_b, gt_h, gb_h, gnorm]
    st_specs = (pl.BlockSpec((bb, hb, dk, dv), lambda b, h, c: (b, h, 0, 0)),
                pl.BlockSpec((bb, hb, 1, dk), lambda b, h, c: (b, h, 0, 0)),
                pl.BlockSpec((bb, hb, 1, 1), lambda b, h, c: (b, h, 0, 0)))
    if not zero_init:
        in_specs.extend(st_specs)
        args.extend(state0)
    in_specs.append(pl.BlockSpec(memory_space=pl.ANY))
    args.append(mix)
    return pl.pallas_call(
        functools.partial(_mlstm_body, q=q, dk=dk, dv=dv, bb=bb, hb=hb, zero_init=zero_init, carry=nc > 1),
        out_shape=(jax.ShapeDtypeStruct(mix.shape, mix.dtype),
                   jax.ShapeDtypeStruct((nb, heads, dk, dv), F32),
                   jax.ShapeDtypeStruct((nb, heads, 1, dk), F32),
                   jax.ShapeDtypeStruct((nb, heads, 1, 1), F32)),
        grid=(nb // bb, heads // hb, nc),
        in_specs=in_specs,
        out_specs=(pl.BlockSpec((rq, hb * dv), lambda b, h, c: (rb0 + b * nc + c, yb0 + h)),) + st_specs,
        input_output_aliases={len(args) - 1: 0},
        compiler_params=_cparams(3),
        name=name,
    )(*args)


def _gla_body(*refs, q, dk, dv, bb, hb, zero_init, carry):
    refs = list(refs)
    q_ref, k_ref, v_ref, r_ref, ga_ref, wgu_ref, bg_ref, gn_ref = refs[:8]
    pos = 8
    s0_ref = None
    if not zero_init:
        s0_ref = refs[pos]
        pos += 1
    y_ref, s_ref = refs[pos + 1:pos + 3]
    _init_carried(carry, zero_init, [(s_ref, s0_ref)])
    tril = _tril(q)
    trilf = tril.astype(F32)
    ones = jnp.ones((q, LANE), F32)
    for bi in range(bb):
        r0 = bi * q
        ga = ga_ref[r0:r0 + q, :]
        for hi in range(hb):
            kh = k_ref[r0:r0 + q, hi * dk:(hi + 1) * dk]
            vh = v_ref[r0:r0 + q, hi * dv:(hi + 1) * dv]
            gate_pre = _dot(ga, wgu_ref[:, hi * dk:(hi + 1) * dk]) + bg_ref[:, hi * dk:(hi + 1) * dk]
            log_a = jax.nn.log_sigmoid(gate_pre) / GLA_TAU
            bcum = _dot_exact_rhs(trilf, log_a)
            qt = q_ref[r0:r0 + q, hi * dk:(hi + 1) * dk] * (dk ** -0.5) * jnp.exp(bcum)
            kt = kh * jnp.exp(-bcum)
            att = jnp.where(tril, _dot(qt, kt, dims=((1,), (1,))), 0.0)
            smat = _read_state(carry, zero_init, s_ref, s0_ref, bi, hi)
            o = _dot(att, vh) + _dot(qt, smat)
            b_last = bcum[q - 1:q, :]
            kw = kh * jnp.exp(b_last - bcum)
            tot = _dot_exact_lhs(log_a, ones, dims=((0,), (0,)))
            decay = jnp.exp(tot[:, 0:1])
            s_ref[bi, hi] = decay * smat + _dot(kw, vh, dims=((0,), (0,)))
            gate_r = _silu(r_ref[r0:r0 + q, hi * dv:(hi + 1) * dv])
            y_ref[r0:r0 + q, hi * dv:(hi + 1) * dv] = (
                _rms(o, gn_ref[:, hi * dv:(hi + 1) * dv]) * gate_r).astype(y_ref.dtype)


def _gla(xw, q_col0, k_col0, v_col0, r_col0, small, wgu, bg, gnorm, s0, mix, row0, nb, seq, q, heads, dk, dv,
         blocking, name):
    nc = seq // q
    bb, hb = _scan_blocking(nb, heads, nc, *blocking)
    rq = bb * q
    kw_, vw_ = hb * dk, hb * dv
    assert row0 % rq == 0 and q_col0 % kw_ == 0 and k_col0 % kw_ == 0 and v_col0 % vw_ == 0 and r_col0 % vw_ == 0
    rb0 = row0 // rq
    qb0, kb0, vb0, rcb0 = q_col0 // kw_, k_col0 // kw_, v_col0 // vw_, r_col0 // vw_
    zero_init = s0 is None
    in_specs = [
        pl.BlockSpec((rq, kw_), lambda b, h, c: (rb0 + b * nc + c, qb0 + h)),
        pl.BlockSpec((rq, kw_), lambda b, h, c: (rb0 + b * nc + c, kb0 + h)),
        pl.BlockSpec((rq, vw_), lambda b, h, c: (rb0 + b * nc + c, vb0 + h)),
        pl.BlockSpec((rq, vw_), lambda b, h, c: (rb0 + b * nc + c, rcb0 + h)),
        pl.BlockSpec((rq, LANE), lambda b, h, c: (rb0 + b * nc + c, 0)),
        pl.BlockSpec((LANE, kw_), lambda b, h, c: (0, h)),
        pl.BlockSpec((1, kw_), lambda b, h, c: (0, h)),
        pl.BlockSpec((1, vw_), lambda b, h, c: (0, h)),
    ]
    args = [xw, xw, xw, xw, small, wgu, bg, gnorm]
    st_spec = pl.BlockSpec((bb, hb, dk, dv), lambda b, h, c: (b, h, 0, 0))
    if not zero_init:
        in_specs.append(st_spec)
        args.append(s0)
    in_specs.append(pl.BlockSpec(memory_space=pl.ANY))
    args.append(mix)
    return pl.pallas_call(
        functools.partial(_gla_body, q=q, dk=dk, dv=dv, bb=bb, hb=hb, zero_init=zero_init, carry=nc > 1),
        out_shape=(jax.ShapeDtypeStruct(mix.shape, mix.dtype),
                   jax.ShapeDtypeStruct((nb, heads, dk, dv), F32)),
        grid=(nb // bb, heads // hb, nc),
        in_specs=in_specs,
        out_specs=(pl.BlockSpec((rq, vw_), lambda b, h, c: (rb0 + b * nc + c, h)), st_spec),
        input_output_aliases={len(args) - 1: 0},
        compiler_params=_cparams(3),
        name=name,
    )(*args)


def _router_body(x_ref, g_ref, wr_ref, idx_ref, gate_ref, *, n_experts):
    xn = _rms(x_ref[...], g_ref[...])
    wr = wr_ref[...]
    xh = xn.astype(BF16)
    xl = (xn - xh.astype(F32)).astype(BF16)
    wh = wr.astype(BF16)
    wl = (wr - wh.astype(F32)).astype(BF16)
    logits = (jnp.dot(xh, wh, preferred_element_type=F32) + jnp.dot(xh, wl, preferred_element_type=F32)
              + jnp.dot(xl, wh, preferred_element_type=F32))
    lane = _iota2(logits.shape, 1)
    neg = -jnp.inf
    l1 = jnp.where(lane < n_experts, logits, neg)
    m1 = jnp.max(l1, axis=1, keepdims=True)
    i1 = jnp.min(jnp.where(l1 == m1, lane, N_EXPERTS_PAD), axis=1, keepdims=True)
    l2 = jnp.where(lane == i1, neg, l1)
    m2 = jnp.max(l2, axis=1, keepdims=True)
    i2 = jnp.min(jnp.where(l2 == m2, lane, N_EXPERTS_PAD), axis=1, keepdims=True)
    e2 = jnp.exp(m2 - m1)
    denom = 1.0 + e2
    idx_ref[...] = jnp.where(lane == 0, i1, jnp.where(lane == 1, i2, 0))
    gate_ref[...] = jnp.where(lane == 0, 1.0 / denom, jnp.where(lane == 1, e2 / denom, 0.0))


def _router(x, g, w_router):
    r, d = x.shape
    n_experts = w_router.shape[1]
    wr = jnp.pad(w_router, ((0, 0), (0, N_EXPERTS_PAD - n_experts)))
    tr = _row_tile(r, 256)
    return pl.pallas_call(
        functools.partial(_router_body, n_experts=n_experts),
        out_shape=(jax.ShapeDtypeStruct((r, N_EXPERTS_PAD), jnp.int32),
                   jax.ShapeDtypeStruct((r, N_EXPERTS_PAD), F32)),
        grid=(r // tr,),
        in_specs=[pl.BlockSpec((tr, d), lambda i: (i, 0)), pl.BlockSpec((1, d), lambda i: (0, 0)),
                  pl.BlockSpec((d, N_EXPERTS_PAD), lambda i: (0, 0))],
        out_specs=(pl.BlockSpec((tr, N_EXPERTS_PAD), lambda i: (i, 0)),
                   pl.BlockSpec((tr, N_EXPERTS_PAD), lambda i: (i, 0))),
        compiler_params=_cparams(1),
        name="moe_router",
    )(x, g.reshape(1, d), wr)


def _row_copy(src_hbm, row, dst_vmem, slot, sem):
    return pltpu.make_async_copy(src_hbm.at[pl.ds(row, 1)], dst_vmem.at[pl.ds(slot, 1)], sem)


DMA_LOOP_UNROLL = 8


def _two_slot_gather(step, n_steps, issue_rows, wait_rows):
    slot = step % 2

    @pl.when(step == 0)
    def _prime():
        issue_rows(0, 0)

    @pl.when(step + 1 < n_steps)
    def _ahead():
        issue_rows(step + 1, 1 - slot)

    wait_rows(slot)
    return slot


def _gather_norm_body(src_ref, x_hbm, g_ref, o_ref, buf_ref, sem, *, tg, n_steps):
    def issue_rows(step, slot):
        base = step * tg

        def body(r, carry):
            _row_copy(x_hbm, src_ref[base + r], buf_ref.at[slot], r, sem.at[slot]).start()
            return carry

        lax.fori_loop(0, tg, body, 0, unroll=DMA_LOOP_UNROLL)

    def wait_rows(slot):
        def body(r, carry):
            _row_copy(x_hbm, 0, buf_ref.at[slot], r, sem.at[slot]).wait()
            return carry

        lax.fori_loop(0, tg, body, 0, unroll=DMA_LOOP_UNROLL)

    slot = _two_slot_gather(pl.program_id(0), n_steps, issue_rows, wait_rows)
    o_ref[...] = _rms(buf_ref[slot], g_ref[...]).astype(o_ref.dtype)


def _gather_norm(x, g, src_rows, tg):
    _, d = x.shape
    p_rows = src_rows.shape[0]
    n_steps = p_rows // tg
    return pl.pallas_call(
        functools.partial(_gather_norm_body, tg=tg, n_steps=n_steps),
        out_shape=jax.ShapeDtypeStruct((p_rows, d), BF16),
        grid_spec=pltpu.PrefetchScalarGridSpec(
            num_scalar_prefetch=1,
            grid=(n_steps,),
            in_specs=[pl.BlockSpec(memory_space=pl.ANY), pl.BlockSpec((1, d), lambda i, s: (0, 0))],
            out_specs=pl.BlockSpec((tg, d), lambda i, s: (i, 0)),
            scratch_shapes=[pltpu.VMEM((2, tg, d), F32), pltpu.SemaphoreType.DMA((2,))],
        ),
        compiler_params=_cparams(1),
        name="moe_gather_norm",
    )(src_rows, x, g.reshape(1, d))


def _combine_body(pos_ref, x_ref, y_hbm, g_ref, oa_ref, ob_ref, buf_ref, sem, *, tc, top_k, na_tiles, n_steps):
    step = pl.program_id(0)

    def issue_rows(s, slot):
        base = s * tc

        def body(r, carry):
            for k in range(top_k):
                _row_copy(y_hbm, pos_ref[(base + r) * top_k + k], buf_ref.at[slot, k], r, sem.at[slot]).start()
            return carry

        lax.fori_loop(0, tc, body, 0, unroll=DMA_LOOP_UNROLL)

    def wait_rows(slot):
        def body(r, carry):
            for k in range(top_k):
                _row_copy(y_hbm, 0, buf_ref.at[slot, k], r, sem.at[slot]).wait()
            return carry

        lax.fori_loop(0, tc, body, 0, unroll=DMA_LOOP_UNROLL)

    slot = _two_slot_gather(step, n_steps, issue_rows, wait_rows)
    y = buf_ref[slot, 0]
    for k in range(1, top_k):
        y = y + buf_ref[slot, k]
    out = _rms(x_ref[...] + y, g_ref[...])

    @pl.when(step < na_tiles)
    def _first():
        oa_ref[...] = out

    @pl.when(step >= na_tiles)
    def _second():
        ob_ref[...] = out


def _combine_norm(x, y_sorted, pos, g, top_k, n_a, n_b):
    _, d = x.shape
    tc = _row_tile(math.gcd(n_a, n_b), 256)
    na_tiles, nb_tiles = n_a // tc, n_b // tc
    return pl.pallas_call(
        functools.partial(_combine_body, tc=tc, top_k=top_k, na_tiles=na_tiles, n_steps=na_tiles + nb_tiles),
        out_shape=(jax.ShapeDtypeStruct((n_a, d), F32), jax.ShapeDtypeStruct((n_b, d), F32)),
        grid_spec=pltpu.PrefetchScalarGridSpec(
            num_scalar_prefetch=1,
            grid=(na_tiles + nb_tiles,),
            in_specs=[pl.BlockSpec((tc, d), lambda i, s: (i, 0)), pl.BlockSpec(memory_space=pl.ANY),
                      pl.BlockSpec((1, d), lambda i, s: (0, 0))],
            out_specs=(pl.BlockSpec((tc, d), lambda i, s: (jnp.minimum(i, na_tiles - 1), 0)),
                       pl.BlockSpec((tc, d), lambda i, s: (jnp.maximum(i - na_tiles, 0), 0))),
            scratch_shapes=[pltpu.VMEM((2, top_k, tc, d), F32), pltpu.SemaphoreType.DMA((2,))],
        ),
        compiler_params=_cparams(1),
        name="moe_combine_norm",
    )(pos, x, y_sorted, g.reshape(1, d))


MOE_ROW_TILE_MAX = 512


def _moe_row_tile(n_assign, n_experts):
    tm = 2 * SUBLANE
    while tm * 2 <= min(MOE_ROW_TILE_MAX, n_assign // (2 * n_experts)):
        tm *= 2
    return tm


def _moe_plan(top_idx, top_gate, n_experts, tm, sub):
    r, top_k = top_idx.shape
    n_assign = r * top_k
    n_tiles = (n_assign + n_experts * (tm - 1)) // tm
    flat_e = top_idx.reshape(-1)
    onehot = (flat_e[:, None] == jnp.arange(n_experts, dtype=jnp.int32)[None, :]).astype(jnp.int32)
    csum = jnp.cumsum(onehot, axis=0)
    counts = csum[-1]
    tiles_per = (counts + tm - 1) // tm
    tile_end = jnp.cumsum(tiles_per)
    tile_start = tile_end - tiles_per
    count_start = jnp.cumsum(counts) - counts
    rank = jnp.sum((csum - onehot) * onehot, axis=1)
    pos = jnp.sum(onehot * (tile_start * tm)[None, :], axis=1) + rank
    total = tile_end[-1]
    t = jnp.arange(n_tiles, dtype=jnp.int32)
    valid = t < total
    t_eff = jnp.minimum(t, total - 1)
    te = jnp.sum((t_eff[:, None] >= tile_end[None, :]).astype(jnp.int32), axis=1)
    te = jnp.minimum(te, n_experts - 1)
    first = valid & (t == tile_start[te])
    occupied = jnp.clip(counts[te] - (t - tile_start[te]) * tm, 0, tm)
    occupied = jnp.where(valid, (occupied + sub - 1) // sub * sub, 0).astype(jnp.int32)
    tiles = (te, first.astype(jnp.int32), occupied, t)
    order = jnp.argsort(flat_e, stable=True).astype(jnp.int32)
    slot = jnp.arange(n_tiles * tm, dtype=jnp.int32)
    slot_e = jnp.repeat(te, tm)
    local = slot - jnp.repeat(tile_start[te] * tm, tm)
    filled = jnp.repeat(valid, tm) & (local < counts[slot_e])
    assign = order[jnp.clip(count_start[slot_e] + local, 0, n_assign - 1)]
    src_rows = jnp.where(filled, assign // top_k, 0)
    row_gate = jnp.where(filled, top_gate.reshape(-1)[assign], 0.0)
    return tiles, pos, src_rows, row_gate.reshape(-1, 1)


def _prev_rows(rows3):
    return jnp.pad(rows3, ((0, 0), (SUBLANE - (CONV_W - 1), 0), (0, 0)))


def kernel(x_prompt, x_sample, state_ssd_conv, state_ssd, state_mlstm_conv, state_mlstm_c, state_mlstm_n,
           state_mlstm_m, state_gla, meta_tokens,
           norm_mix_even, w_in_even, ssd_conv_w, ssd_conv_b, ssd_dt_bias, ssd_a_log, ssd_d, ssd_norm,
           mlstm_conv_w, mlstm_conv_b, mlstm_if_bias, mlstm_norm, w_out_even,
           norm_ffn_even, ffn_w_gate, ffn_w_up, ffn_w_down,
           norm_mix_odd, w_in_odd, gla_w_gate_up, gla_b_gate, gla_norm, w_out_odd,
           norm_ffn_odd, moe_router, moe_w1, moe_w3, moe_w2, final_norm):
    bp, sp, d = x_prompt.shape
    bs, ss, _ = x_sample.shape
    n_meta = meta_tokens.shape[0]
    assert w_in_even.shape[0] == 1 and w_in_odd.shape[0] == 1, "one even and one odd layer"

    ssd_heads = ssd_a_log.shape[-1]
    d_inner = ssd_norm.shape[-1]
    p_dim = d_inner // ssd_heads
    conv_dim = ssd_conv_w.shape[-1]
    n_state = (conv_dim - d_inner) // (2 * SSD_GROUPS)
    hpg = ssd_heads // SSD_GROUPS
    ml_heads = mlstm_if_bias.shape[-1] // 2
    ml_v = mlstm_norm.shape[-1]
    ml_dv = ml_v // ml_heads
    ml_qk = mlstm_conv_w.shape[-1] // 2
    ml_dk = ml_qk // ml_heads
    gla_heads, gla_dk, gla_dv = state_gla.shape[2:]
    gla_rank = gla_w_gate_up.shape[1]
    n_experts = moe_router.shape[-1]
    top_k = 2

    n_p, n_s, n_m = bp * sp, bs * ss, bp * n_meta
    rows = n_p + n_s + n_m
    row_p, row_s, row_m = 0, n_p, n_p + n_s
    seg_m = (row_m, bp, n_meta, (n_meta,) * 3)
    seg_p = (row_p, bp, sp, PROMPT_CHUNKS)
    seg_s = (row_s, bs, ss, (ss,) * 3)

    meta_rows = jnp.broadcast_to(meta_tokens.astype(F32)[None], (bp, n_meta, d)).reshape(n_m, d)
    x0 = jnp.concatenate([x_prompt.reshape(n_p, d), x_sample.reshape(n_s, d), meta_rows], axis=0)

    xn = _rmsnorm(x0, norm_mix_even[0])
    w_in = w_in_even
    off_z, off_xbc = 0, d_inner
    off_dt = off_xbc + conv_dim
    off_qk = off_dt + ssd_heads
    off_v = off_qk + 2 * ml_qk
    off_if = off_v + ml_v
    off_o = off_if + 2 * ml_heads
    wt_in = jnp.swapaxes(w_in[0], 0, 1)
    tn_in = _pick(math.gcd(math.gcd(off_dt, 2 * ml_qk + ml_v), ml_v), (512, 256, 128))
    slabs = ([o for o in range(0, off_dt, tn_in)] + [o for o in range(off_qk, off_if, tn_in)]
             + [o for o in range(off_o, off_o + ml_v, tn_in)])
    xw = _dense_mm_nt(xn, wt_in, slabs, tn_in, "in_even")
    col_qk = off_dt
    col_v = col_qk + 2 * ml_qk
    col_o = col_v + ml_v
    n_small = ssd_heads + 2 * ml_heads
    wt_s = jnp.concatenate([wt_in[off_dt:off_qk], wt_in[off_if:off_o]], axis=0)
    wt_s = jnp.pad(wt_s, ((0, LANE - n_small), (0, 0)))
    small = _dense_mm_nt(xn, wt_s, [0], LANE, "in_even_small")

    dt_g = small[:, :ssd_heads].reshape(rows, SSD_GROUPS, hpg).transpose(1, 0, 2)
    par_g = jnp.stack([ssd_dt_bias[0], ssd_a_log[0], ssd_d[0]], axis=0).reshape(3, SSD_GROUPS, hpg)
    par_g = par_g.transpose(1, 0, 2)
    gt_h = small[:, ssd_heads:n_small].reshape(rows, 2, ml_heads).transpose(2, 0, 1)
    gb_h = mlstm_if_bias[0].reshape(2, ml_heads).transpose(1, 0).reshape(ml_heads, 1, 2)
    ssd_gn = ssd_norm[0].reshape(1, d_inner)
    ml_gn = mlstm_norm[0].reshape(1, ml_v)

    def even_segment(seg, prev_ssd, prev_ml, st, mix):
        row0, nb, seq, qs = seg
        xbc_c, tail_ssd = _conv_silu(xw, off_xbc, conv_dim, row0, nb, seq, prev_ssd,
                                     ssd_conv_w[0], ssd_conv_b[0], "ssd_conv")
        qk_c, tail_ml = _conv_silu(xw, col_qk, 2 * ml_qk, row0, nb, seq, prev_ml,
                                   mlstm_conv_w[0], mlstm_conv_b[0], "mlstm_conv")
        h0 = None if st is None else st[0]
        mix, h_new = _ssd(xbc_c, xw, off_z, dt_g, par_g, ssd_gn, h0, mix, 0, row0, nb, seq,
                          math.gcd(seq, qs[0]), hpg, p_dim, n_state, SSD_BLOCKING, "ssd_scan")
        ml0 = None if st is None else st[1:]
        mix, c_new, n_new, m_new = _mlstm(qk_c, xw, col_v, col_o, gt_h, gb_h, ml_gn, ml0,
                                          mix, d_inner, row0, nb, seq, math.gcd(seq, qs[1]), ml_heads, ml_dk,
                                          ml_dv, MLSTM_BLOCKING, "mlstm_scan")
        return mix, (h_new, c_new, n_new, m_new), tail_ssd, tail_ml

    zeros_ssd = jnp.zeros((bp, SUBLANE, conv_dim), F32)
    zeros_ml = jnp.zeros((bp, SUBLANE, 2 * ml_qk), F32)
    mix = jnp.zeros((rows, d_inner + ml_v), BF16)
    mix, st_m, cs_m, cm_m = even_segment(seg_m, zeros_ssd, zeros_ml, None, mix)
    mix, st_p, cs_p, cm_p = even_segment(seg_p, cs_m, cm_m, st_m, mix)
    st_s0 = (state_ssd[0].reshape(bs, SSD_GROUPS, hpg * p_dim, n_state), state_mlstm_c[0],
             state_mlstm_n[0].reshape(bs, ml_heads, 1, ml_dk), state_mlstm_m[0].reshape(bs, ml_heads, 1, 1))
    mix, st_s, cs_s, cm_s = even_segment(seg_s, _prev_rows(state_ssd_conv[0]), _prev_rows(state_mlstm_conv[0]),
                                         st_s0, mix)

    x1 = _dense_mm(mix, w_out_even, res=x0, name="out_even")
    xn = _rmsnorm(x1, norm_ffn_even[0])
    hid = _dense_swiglu(xn, ffn_w_gate, ffn_w_up, "ffn_up")
    x2 = _dense_mm(hid, ffn_w_down, res=x1, name="ffn_down")

    xn = _rmsnorm(x2, norm_mix_odd[0])
    n_main = 2 * gla_heads * gla_dk + 2 * gla_heads * gla_dv
    wt_odd = jnp.swapaxes(w_in_odd[0], 0, 1)
    tn_odd = _pick(n_main, (512, 256, 128))
    xw_o = _dense_mm_nt(xn, wt_odd, list(range(0, n_main, tn_odd)), tn_odd, "in_odd")
    wt_ga = jnp.pad(wt_odd[n_main:n_main + gla_rank], ((0, LANE - gla_rank), (0, 0)))
    small_o = _dense_mm_nt(xn, wt_ga, [0], LANE, "in_odd_small")
    gla_gn = gla_norm[0].reshape(1, gla_heads * gla_dv)
    bg = gla_b_gate[0].reshape(1, gla_heads * gla_dk)
    wgu_pad = jnp.pad(gla_w_gate_up[0], ((0, LANE - gla_rank), (0, 0)))
    k_col0 = gla_heads * gla_dk
    v_col0 = 2 * gla_heads * gla_dk
    r_col0 = v_col0 + gla_heads * gla_dv

    def odd_segment(seg, s0, mix):
        row0, nb, seq, qs = seg
        return _gla(xw_o, 0, k_col0, v_col0, r_col0, small_o, wgu_pad, bg, gla_gn, s0, mix, row0, nb, seq,
                    math.gcd(seq, qs[2]), gla_heads, gla_dk, gla_dv, GLA_BLOCKING, "gla_scan")

    mix = jnp.zeros((rows, gla_heads * gla_dv), BF16)
    mix, s_m = odd_segment(seg_m, None, mix)
    mix, s_p = odd_segment(seg_p, s_m, mix)
    mix, s_s = odd_segment(seg_s, state_gla[0], mix)
    x3 = _dense_mm(mix, w_out_odd, res=x2, name="out_odd")

    idx_pad, gate_pad = _router(x3, norm_ffn_odd[0], moe_router[0])
    tm_e = _moe_row_tile(rows * top_k, n_experts)
    sub_e = max(2 * SUBLANE, tm_e // 4)
    tiles_e, pos, src_rows, row_gate = _moe_plan(idx_pad[:, :top_k], gate_pad[:, :top_k], n_experts, tm_e, sub_e)
    xs = _gather_norm(x3, norm_ffn_odd[0], src_rows, _pick(tm_e, (256, 128, 64, 32, 16, 8)))
    _, tn_up = _plan_swiglu(d, moe_w1.shape[-1], (tm_e,))
    hs = _gmm_swiglu(xs, moe_w1[0], moe_w3[0], tiles_e, tm=tm_e, tn=tn_up, sub=sub_e, name="moe_up")
    _, tn_dn, tk_dn = _plan_mm(moe_w2.shape[-2], d, 0, (tm_e,), False)
    ys = _gmm(hs, moe_w2[0], tiles_e, tm=tm_e, tn=tn_dn, tk=tk_dn, sub=sub_e, scale=row_gate, name="moe_down")
    y_prompt, y_sample = _combine_norm(x3, ys, pos, final_norm, top_k, n_p, n_s)
    y_prompt = y_prompt.reshape(bp, sp, d)
    y_sample = y_sample.reshape(bs, ss, d)

    def pack_even(cs, cm, st, nb):
        h_new, c_new, n_new, m_new = st
        keep = SUBLANE - (CONV_W - 1)
        return (cs[None, :, keep:, :], h_new.reshape(1, nb, ssd_heads, p_dim, n_state), cm[None, :, keep:, :],
                c_new[None],
                n_new.reshape(1, nb, ml_heads, ml_dk), m_new.reshape(1, nb, ml_heads))

    return ((y_prompt, y_sample) + pack_even(cs_p, cm_p, st_p, bp) + (s_p[None],)
            + pack_even(cs_s, cm_s, st_s, bs) + (s_s[None],))
```

```python
import functools
import math

import jax
import jax.numpy as jnp
from jax import lax
from jax.experimental import pallas as pl
from jax.experimental.pallas import tpu as pltpu

F32 = jnp.float32
BF16 = jnp.bfloat16

EPS = 1e-6
CONV_W = 4
SSD_GROUPS = 8
GLA_TAU = 16.0
N_EXPERTS_PAD = 128
PROMPT_CHUNKS = (128, 256, 64)
SSD_BLOCKING = (4, 2)
MLSTM_BLOCKING = (4, 2)
GLA_BLOCKING = (1, 2)

V7X_VMEM_BYTES = 64 * 1024 * 1024
VMEM_LIMIT = V7X_VMEM_BYTES - 8 * 1024 * 1024
VMEM_BUDGET = VMEM_LIMIT - 4 * 1024 * 1024
LANE = 128
SUBLANE = 8
TK_MAX = 4096


def _cparams(n_axes):
    return pltpu.CompilerParams(dimension_semantics=("arbitrary",) * n_axes, vmem_limit_bytes=VMEM_LIMIT)


def _pick(n, candidates):
    for c in candidates:
        if c <= n and n % c == 0:
            return c
    return n


def _row_tile(n, cap):
    if n <= cap:
        return n
    best = SUBLANE
    for c in range(SUBLANE, cap + 1, SUBLANE):
        if n % c == 0:
            best = c
    return best


def _dot(a, b, dims=((1,), (0,))):
    return lax.dot_general(a.astype(BF16), b.astype(BF16), (dims, ((), ())), preferred_element_type=F32)


def _split3(x):
    hi = x.astype(BF16)
    r1 = x - hi.astype(F32)
    mid = r1.astype(BF16)
    lo = (r1 - mid.astype(F32)).astype(BF16)
    return hi, mid, lo


def _dot_exact_rhs(a01, b, dims=((1,), (0,))):
    a = a01.astype(BF16)
    out = None
    for part in _split3(b):
        t = lax.dot_general(a, part, (dims, ((), ())), preferred_element_type=F32)
        out = t if out is None else out + t
    return out


def _dot_exact_lhs(a, b01, dims=((1,), (0,))):
    b = b01.astype(BF16)
    out = None
    for part in _split3(a):
        t = lax.dot_general(part, b, (dims, ((), ())), preferred_element_type=F32)
        out = t if out is None else out + t
    return out


def _iota2(shape, axis):
    return lax.broadcasted_iota(jnp.int32, shape, axis)


def _tril(q):
    return _iota2((q, q), 0) >= _iota2((q, q), 1)


def _transpose_small(x, q):
    eye = (_iota2((q, q), 0) == _iota2((q, q), 1)).astype(F32)
    return _dot_exact_lhs(x, eye, dims=((0,), (0,)))


def _silu(x):
    return x * jax.nn.sigmoid(x)


def _rms(x, g):
    return x * lax.rsqrt(jnp.mean(x * x, axis=-1, keepdims=True) + EPS) * g


def _rmsnorm_body(x_ref, g_ref, o_ref):
    o_ref[...] = _rms(x_ref[...], g_ref[...]).astype(o_ref.dtype)


def _rmsnorm(x, g, out_dtype=BF16):
    r, d = x.shape
    tr = _row_tile(r, 256)
    return pl.pallas_call(
        _rmsnorm_body,
        out_shape=jax.ShapeDtypeStruct((r, d), out_dtype),
        grid=(r // tr,),
        in_specs=[pl.BlockSpec((tr, d), lambda i: (i, 0)), pl.BlockSpec((1, d), lambda i: (0, 0))],
        out_specs=pl.BlockSpec((tr, d), lambda i: (i, 0)),
        compiler_params=_cparams(1),
        name="rmsnorm",
    )(x, g.reshape(1, d))


def _tile_rows_dispatch(nrows, tm, sub, rows_block, zero_block):
    @pl.when(nrows == tm)
    def _full():
        rows_block(0, tm)

    if sub is None:
        @pl.when(nrows == 0)
        def _empty():
            zero_block(0, tm)
        return
    for s in range(tm // sub):
        @pl.when((nrows < tm) & (nrows > s * sub))
        def _piece(s=s):
            rows_block(s * sub, sub)

        @pl.when(nrows <= s * sub)
        def _zero(s=s):
            zero_block(s * sub, sub)


def _gmm_body(te_ref, tf_ref, tn_ref, tr_ref, x_ref, w_ref, *rest, kc, tm, sub, has_res, has_scale):
    del te_ref, tr_ref
    rest = list(rest)
    res_ref = rest.pop(0) if has_res else None
    scale_ref = rest.pop(0) if has_scale else None
    o_ref, wbf_ref = rest[0], rest[1]
    acc_ref = rest[2] if kc > 1 else None
    t = pl.program_id(1)
    c = pl.program_id(2)

    @pl.when(tf_ref[t] == 1)
    def _cast():
        wbf_ref[c] = w_ref[...].astype(BF16)

    def rows_block(r0, n):
        rs = slice(r0, r0 + n)

        def finish(v):
            if has_scale:
                v = v * scale_ref[rs, :]
            if has_res:
                v = v + res_ref[rs, :]
            o_ref[rs, :] = v.astype(o_ref.dtype)

        part = jnp.dot(x_ref[rs, :], wbf_ref[c], preferred_element_type=F32)
        if kc == 1:
            finish(part)
        else:
            @pl.when(c == 0)
            def _():
                acc_ref[rs, :] = part

            @pl.when(c > 0)
            def _():
                acc_ref[rs, :] += part

            @pl.when(c == kc - 1)
            def _():
                finish(acc_ref[rs, :])

    def zero_block(r0, n):
        o_ref[r0:r0 + n, :] = jnp.zeros((n, o_ref.shape[1]), o_ref.dtype)

    _tile_rows_dispatch(tn_ref[t], tm, sub, rows_block, zero_block)


def _dense_tiles(n_tiles, tm):
    t = jnp.arange(n_tiles, dtype=jnp.int32)
    return (jnp.zeros((n_tiles,), jnp.int32), (t == 0).astype(jnp.int32), jnp.full((n_tiles,), tm, jnp.int32), t)


def _tk_options(k):
    if k <= TK_MAX:
        return [k]
    divs = [c for c in range(LANE, k, LANE) if k % c == 0]
    good = sorted((c for c in divs if TK_MAX // 2 <= c <= TK_MAX), reverse=True)
    over = sorted(c for c in divs if TK_MAX < c <= 2 * TK_MAX)
    return good + over or sorted(divs, reverse=True)


def _tm_options(rows):
    opts = [c for c in range(2 * SUBLANE, min(rows, 1024) + 1, 2 * SUBLANE) if rows % c == 0]
    return sorted(opts, reverse=True) or [rows]


def _plan_mm(k, n_cols, col0, tm_options, has_res):
    for tn in (512, 256, 128):
        if n_cols % tn or col0 % tn:
            continue
        for tm in tm_options:
            for tk in _tk_options(k):
                kc = k // tk
                need = (2 * tk * tn * 4 + k * tn * 2 + 2 * tm * tk * 2 + 3 * tm * tn * 4
                        + (2 * tm * tn * 4 if has_res else 0) + (tm * tn * 4 if kc > 1 else 0))
                if need <= VMEM_BUDGET:
                    return tm, tn, tk
    raise ValueError(f"no matmul tiling fits VMEM for k={k} n={n_cols}")


def _plan_swiglu(k, f, tm_options):
    for tn in (512, 256, 128):
        if f % tn:
            continue
        for tm in tm_options:
            need = 2 * (2 * k * tn * 4) + 2 * k * tn * 2 + 2 * tm * k * 2 + 2 * tm * tn * 2 + 3 * tm * tn * 4
            if need <= VMEM_BUDGET:
                return tm, tn
    raise ValueError(f"no swiglu tiling fits VMEM for k={k} f={f}")


def _gmm(x, w, tiles, *, tm, tn, tk, sub=None, n_cols=None, col0=0, res=None, scale=None, out_dtype=F32,
         name="gmm"):
    te, tf, tv, tr = tiles
    n_tiles = te.shape[0]
    rows, k = x.shape
    n_total = w.shape[2]
    n_cols = n_total if n_cols is None else n_cols
    assert col0 % tn == 0 and n_cols % tn == 0 and rows % tm == 0 and k % tk == 0
    assert sub is None or tm % sub == 0
    kc = k // tk
    j0 = col0 // tn

    def x_map(j, t, c, te, tf, tv, tr):
        return (tr[t], jnp.where(tv[t] > 0, c, kc - 1))

    def w_map(j, t, c, te, tf, tv, tr):
        return (te[t], jnp.where(tf[t] == 1, c, kc - 1), j0 + j)

    def o_map(j, t, c, te, tf, tv, tr):
        return (tr[t], j)

    in_specs = [pl.BlockSpec((tm, tk), x_map), pl.BlockSpec((None, tk, tn), w_map)]
    args = [x, w]
    if res is not None:
        in_specs.append(pl.BlockSpec((tm, tn), o_map))
        args.append(res)
    if scale is not None:
        in_specs.append(pl.BlockSpec((tm, 1), lambda j, t, c, te, tf, tv, tr: (tr[t], 0)))
        args.append(scale)
    scratch = [pltpu.VMEM((kc, tk, tn), BF16)]
    if kc > 1:
        scratch.append(pltpu.VMEM((tm, tn), F32))
    return pl.pallas_call(
        functools.partial(_gmm_body, kc=kc, tm=tm, sub=sub, has_res=res is not None,
                          has_scale=scale is not None),
        out_shape=jax.ShapeDtypeStruct((rows, n_cols), out_dtype),
        grid_spec=pltpu.PrefetchScalarGridSpec(
            num_scalar_prefetch=4,
            grid=(n_cols // tn, n_tiles, kc),
            in_specs=in_specs,
            out_specs=pl.BlockSpec((tm, tn), o_map),
            scratch_shapes=scratch,
        ),
        compiler_params=_cparams(3),
        name=name,
    )(te, tf, tv, tr, *args)


def _gmm_swiglu_body(te_ref, tf_ref, tn_ref, tr_ref, x_ref, wg_ref, wu_ref, o_ref, wgb_ref, wub_ref, *, tm, sub):
    del te_ref, tr_ref
    t = pl.program_id(1)

    @pl.when(tf_ref[t] == 1)
    def _cast():
        wgb_ref[...] = wg_ref[...].astype(BF16)
        wub_ref[...] = wu_ref[...].astype(BF16)

    def rows_block(r0, n):
        x = x_ref[r0:r0 + n, :]
        g = jnp.dot(x, wgb_ref[...], preferred_element_type=F32)
        u = jnp.dot(x, wub_ref[...], preferred_element_type=F32)
        o_ref[r0:r0 + n, :] = (_silu(g) * u).astype(o_ref.dtype)

    def zero_block(r0, n):
        o_ref[r0:r0 + n, :] = jnp.zeros((n, o_ref.shape[1]), o_ref.dtype)

    _tile_rows_dispatch(tn_ref[t], tm, sub, rows_block, zero_block)


def _gmm_swiglu(x, wg, wu, tiles, *, tm, tn, sub=None, name="gmm_swiglu"):
    te, tf, tv, tr = tiles
    n_tiles = te.shape[0]
    rows, k = x.shape
    f = wg.shape[2]
    assert rows % tm == 0 and f % tn == 0

    def x_map(j, t, te, tf, tv, tr):
        return (tr[t], 0)

    def w_map(j, t, te, tf, tv, tr):
        return (te[t], 0, j)

    def o_map(j, t, te, tf, tv, tr):
        return (tr[t], j)

    return pl.pallas_call(
        functools.partial(_gmm_swiglu_body, tm=tm, sub=sub),
        out_shape=jax.ShapeDtypeStruct((rows, f), BF16),
        grid_spec=pltpu.PrefetchScalarGridSpec(
            num_scalar_prefetch=4,
            grid=(f // tn, n_tiles),
            in_specs=[pl.BlockSpec((tm, k), x_map), pl.BlockSpec((None, k, tn), w_map),
                      pl.BlockSpec((None, k, tn), w_map)],
            out_specs=pl.BlockSpec((tm, tn), o_map),
            scratch_shapes=[pltpu.VMEM((k, tn), BF16), pltpu.VMEM((k, tn), BF16)],
        ),
        compiler_params=_cparams(2),
        name=name,
    )(te, tf, tv, tr, x, wg, wu)


def _dense_mm(x, w, *, n_cols=None, col0=0, res=None, name):
    rows, k = x.shape
    n = w.shape[2] if n_cols is None else n_cols
    tm, tn, tk = _plan_mm(k, n, col0, _tm_options(rows), res is not None)
    return _gmm(x, w, _dense_tiles(rows // tm, tm), tm=tm, tn=tn, tk=tk, n_cols=n_cols, col0=col0, res=res,
                name=name)


def _mm_nt_body(off_ref, x_ref, w_ref, o_ref, wbf_ref):
    del off_ref

    @pl.when(pl.program_id(1) == 0)
    def _cast():
        wbf_ref[...] = w_ref[...].astype(BF16)

    o_ref[...] = lax.dot_general(x_ref[...], wbf_ref[...], (((1,), (1,)), ((), ())), preferred_element_type=F32)


def _dense_mm_nt(x, wt, row_offsets, tn, name):
    rows, k = x.shape
    nj = len(row_offsets)
    assert all(o % SUBLANE == 0 and o + tn <= wt.shape[0] for o in row_offsets)
    tm = next(t for t in _tm_options(rows)
              if 2 * tn * k * 4 + tn * k * 2 + 2 * t * k * 2 + 3 * t * tn * 4 <= VMEM_BUDGET)
    return pl.pallas_call(
        _mm_nt_body,
        out_shape=jax.ShapeDtypeStruct((rows, nj * tn), F32),
        grid_spec=pltpu.PrefetchScalarGridSpec(
            num_scalar_prefetch=1,
            grid=(nj, rows // tm),
            in_specs=[pl.BlockSpec((tm, k), lambda j, i, off: (i, 0)),
                      pl.BlockSpec((pl.Element(tn), pl.Element(k)),
                                   lambda j, i, off: (pl.multiple_of(off[j] * SUBLANE, SUBLANE), 0))],
            out_specs=pl.BlockSpec((tm, tn), lambda j, i, off: (i, j)),
            scratch_shapes=[pltpu.VMEM((tn, k), BF16)],
        ),
        compiler_params=_cparams(2),
        name=name,
    )(jnp.asarray([o // SUBLANE for o in row_offsets], jnp.int32), x, wt)


def _dense_swiglu(x, wg, wu, name):
    rows, k = x.shape
    tm, tn = _plan_swiglu(k, wg.shape[2], _tm_options(rows))
    return _gmm_swiglu(x, wg, wu, _dense_tiles(rows // tm, tm), tm=tm, tn=tn, name=name)


def _conv_body(prev_ref, u_ref, w_ref, b_ref, o_ref, tail_ref, full_ref, *, seq, bb):
    for bi in range(bb):
        r0 = bi * seq
        f0 = bi * (SUBLANE + seq)
        full_ref[f0:f0 + SUBLANE, :] = prev_ref[bi]
        full_ref[f0 + SUBLANE:f0 + SUBLANE + seq, :] = u_ref[r0:r0 + seq, :]
        tail_ref[bi] = full_ref[f0 + seq:f0 + seq + SUBLANE, :]
        acc = b_ref[...]
        for k in range(CONV_W):
            start = f0 + SUBLANE - (CONV_W - 1) + k
            acc = acc + full_ref[start:start + seq, :] * w_ref[k:k + 1, :]
        o_ref[r0:r0 + seq, :] = _silu(acc)


CONV_BLOCK_BYTES = 2 * 1024 * 1024


def _conv_silu(u_arr, col0, c, row0, nb, seq, prev, w, b, name):
    ct = LANE
    for cand in (2048, 1024, 512, 256):
        if c % cand == 0 and col0 % cand == 0 and seq * cand * 4 <= CONV_BLOCK_BYTES:
            ct = cand
            break
    bb = 1
    while bb * 2 <= 16 and nb % (bb * 2) == 0 and bb * 2 * seq * ct * 4 <= CONV_BLOCK_BYTES:
        bb *= 2
    assert col0 % ct == 0 and row0 % (bb * seq) == 0
    rb0, cb0 = row0 // (bb * seq), col0 // ct
    return pl.pallas_call(
        functools.partial(_conv_body, seq=seq, bb=bb),
        out_shape=(jax.ShapeDtypeStruct((nb * seq, c), F32), jax.ShapeDtypeStruct((nb, SUBLANE, c), F32)),
        grid=(nb // bb, c // ct),
        in_specs=[pl.BlockSpec((bb, SUBLANE, ct), lambda b_, j: (b_, 0, j)),
                  pl.BlockSpec((bb * seq, ct), lambda b_, j: (rb0 + b_, cb0 + j)),
                  pl.BlockSpec((CONV_W, ct), lambda b_, j: (0, j)),
                  pl.BlockSpec((1, ct), lambda b_, j: (0, j))],
        out_specs=(pl.BlockSpec((bb * seq, ct), lambda b_, j: (b_, j)),
                   pl.BlockSpec((bb, SUBLANE, ct), lambda b_, j: (b_, 0, j))),
        scratch_shapes=[pltpu.VMEM((bb * (SUBLANE + seq), ct), F32)],
        compiler_params=_cparams(2),
        name=name,
    )(prev, u_arr, w, b.reshape(1, c))


def _scan_blocking(nb, units, nc, bb_max, ub_max, col_offsets=(), unit_width=1):
    bb = 1
    if nc == 1:
        while bb * 2 <= bb_max and nb % (bb * 2) == 0:
            bb *= 2
    ub = 1
    while (ub * 2 <= ub_max and units % (ub * 2) == 0
           and all(c % (ub * 2 * unit_width) == 0 for c in col_offsets)):
        ub *= 2
    return bb, ub


def _init_carried(carry, zero_init, pairs):
    if not carry:
        return

    @pl.when(pl.program_id(2) == 0)
    def _init():
        for out_ref, in_ref in pairs:
            out_ref[...] = jnp.zeros_like(out_ref) if zero_init else in_ref[...]


def _read_state(carry, zero_init, out_ref, in_ref, bi, ui):
    if carry:
        return out_ref[bi, ui]
    if zero_init:
        return jnp.zeros(out_ref.shape[2:], out_ref.dtype)
    return in_ref[bi, ui]


def _ssd_body(*refs, q, hpg, p, bb, gb, zero_init, carry):
    refs = list(refs)
    x_ref, b_ref, c_ref, z_ref, dt_ref, par_ref, gn_ref = refs[:7]
    pos = 7
    h0_ref = None
    if not zero_init:
        h0_ref = refs[pos]
        pos += 1
    y_ref, h_ref, ys_ref = refs[pos + 1:pos + 4]
    _init_carried(carry, zero_init, [(h_ref, h0_ref)])

    gw = hpg * p
    n = b_ref.shape[1] // gb
    tril = _tril(q)
    trilf = tril.astype(F32)
    expand = (_iota2((hpg, gw), 1) // p == _iota2((hpg, gw), 0)).astype(F32)
    expand_t = (_iota2((gw, hpg), 0) // p == _iota2((gw, hpg), 1)).astype(F32)
    for bi in range(bb):
        r0 = bi * q
        for gi in range(gb):
            c0 = gi * gw
            x = x_ref[r0:r0 + q, c0:c0 + gw]
            bm = b_ref[r0:r0 + q, gi * n:(gi + 1) * n]
            cm = c_ref[r0:r0 + q, gi * n:(gi + 1) * n]
            par = par_ref[gi]
            dt = jax.nn.softplus(dt_ref[gi, r0:r0 + q, :] + par[0:1, :])
            a = -jnp.exp(par[1:2, :])
            dta = dt * a
            cum = _dot_exact_rhs(trilf, dta)
            cum_t = _transpose_small(cum, q)
            dt_t = _transpose_small(dt, q)
            hstate = _read_state(carry, zero_init, h_ref, h0_ref, bi, gi)
            cb = _dot(cm, bm, dims=((1,), (1,)))
            y_inter = _dot(cm, hstate, dims=((1,), (1,)))
            cum_last = cum[q - 1:q, :]
            w_end = jnp.exp(cum_last - cum) * dt
            ecum_w = _dot_exact_lhs(jnp.exp(cum), expand)
            wend_w = _dot_exact_lhs(w_end, expand)
            d_w = _dot_exact_lhs(par[2:3, :], expand)
            for h in range(hpg):
                seg = cum[:, h:h + 1] - cum_t[h:h + 1, :]
                lmat = jnp.exp(jnp.where(tril, seg, -jnp.inf))
                scores = cb * lmat * dt_t[h:h + 1, :]
                ys_ref[r0:r0 + q, c0 + h * p:c0 + (h + 1) * p] = _dot(scores, x[:, h * p:(h + 1) * p])
            y = ys_ref[r0:r0 + q, c0:c0 + gw] + y_inter * ecum_w + d_w * x
            y = y * _silu(z_ref[r0:r0 + q, c0:c0 + gw])
            y_ref[r0:r0 + q, c0:c0 + gw] = _rms(y, gn_ref[:, c0:c0 + gw]).astype(y_ref.dtype)
            last_col = jnp.broadcast_to(cum_t[:, q - 1:q], (hpg, LANE))
            decay = jnp.exp(_dot_exact_rhs(expand_t, last_col))
            if n != LANE:
                decay = jnp.broadcast_to(decay[:, 0:1], hstate.shape)
            h_ref[bi, gi] = decay * hstate + _dot(x * wend_w, bm, dims=((0,), (0,)))


def _ssd(xbc_c, xw_a, z_col0, dt_g, par_g, gnorm, h0, mix, y_col0, row0, nb, seq, q, hpg, p, n, blocking, name):
    g_cnt = SSD_GROUPS
    nc = seq // q
    gw = hpg * p
    d_inner = g_cnt * gw
    bb, gb = _scan_blocking(nb, g_cnt, nc, *blocking)
    rq, cw = bb * q, gb * gw
    assert row0 % rq == 0 and z_col0 % cw == 0 and y_col0 % cw == 0 and d_inner % (gb * n) == 0
    rb0 = row0 // rq
    zb0, yb0 = z_col0 // cw, y_col0 // cw
    bb0 = d_inner // (gb * n)
    cb0 = (d_inner + g_cnt * n) // (gb * n)
    zero_init = h0 is None

    in_specs = [
        pl.BlockSpec((rq, cw), lambda b, g, c: (b * nc + c, g)),
        pl.BlockSpec((rq, gb * n), lambda b, g, c: (b * nc + c, bb0 + g)),
        pl.BlockSpec((rq, gb * n), lambda b, g, c: (b * nc + c, cb0 + g)),
        pl.BlockSpec((rq, cw), lambda b, g, c: (rb0 + b * nc + c, zb0 + g)),
        pl.BlockSpec((gb, rq, hpg), lambda b, g, c: (g, rb0 + b * nc + c, 0)),
        pl.BlockSpec((gb, 3, hpg), lambda b, g, c: (g, 0, 0)),
        pl.BlockSpec((1, cw), lambda b, g, c: (0, g)),
    ]
    args = [xbc_c, xbc_c, xbc_c, xw_a, dt_g, par_g, gnorm]
    st_spec = pl.BlockSpec((bb, gb, gw, n), lambda b, g, c: (b, g, 0, 0))
    if not zero_init:
        in_specs.append(st_spec)
        args.append(h0)
    in_specs.append(pl.BlockSpec(memory_space=pl.ANY))
    args.append(mix)
    return pl.pallas_call(
        functools.partial(_ssd_body, q=q, hpg=hpg, p=p, bb=bb, gb=gb, zero_init=zero_init, carry=nc > 1),
        out_shape=(jax.ShapeDtypeStruct(mix.shape, mix.dtype),
                   jax.ShapeDtypeStruct((nb, g_cnt, gw, n), F32)),
        grid=(nb // bb, g_cnt // gb, nc),
        in_specs=in_specs,
        out_specs=(pl.BlockSpec((rq, cw), lambda b, g, c: (rb0 + b * nc + c, yb0 + g)), st_spec),
        scratch_shapes=[pltpu.VMEM((rq, cw), F32)],
        input_output_aliases={len(args) - 1: 0},
        compiler_params=_cparams(3),
        name=name,
    )(*args)


def _mlstm_body(*refs, q, dk, dv, bb, hb, zero_init, carry):
    refs = list(refs)
    q_ref, k_ref, v_ref, o_ref, gt_ref, gb_ref, gn_ref = refs[:7]
    pos = 7
    c0_ref = n0_ref = m0_ref = None
    if not zero_init:
        c0_ref, n0_ref, m0_ref = refs[pos:pos + 3]
        pos += 3
    y_ref, c_ref, n_ref, m_ref = refs[pos + 1:pos + 5]
    _init_carried(carry, zero_init, [(c_ref, c0_ref), (n_ref, n0_ref), (m_ref, m0_ref)])

    tril = _tril(q)
    trilf = tril.astype(F32)
    lane0 = _iota2((q, LANE), 1) == 0
    for bi in range(bb):
        r0 = bi * q
        for hi in range(hb):
            qh = q_ref[r0:r0 + q, hi * dk:(hi + 1) * dk] * (dk ** -0.5)
            kh = k_ref[r0:r0 + q, hi * dk:(hi + 1) * dk]
            vh = v_ref[r0:r0 + q, hi * dv:(hi + 1) * dv]
            gates = gt_ref[hi, r0:r0 + q, :] + gb_ref[hi]
            log_i = gates[:, 0:1]
            log_f = jax.nn.log_sigmoid(gates[:, 1:2])
            cumf = _dot_exact_rhs(trilf, jnp.broadcast_to(log_f, (q, LANE)))[:, 0:1]
            rows = _transpose_small(jnp.where(lane0, cumf, log_i), q)
            cumf_r, logi_r = rows[0:1, :], rows[1:2, :]
            m_prev = _read_state(carry, zero_init, m_ref, m0_ref, bi, hi)
            cmat = _read_state(carry, zero_init, c_ref, c0_ref, bi, hi)
            nvec = _read_state(carry, zero_init, n_ref, n0_ref, bi, hi)
            dmat = jnp.where(tril, cumf - cumf_r + logi_r, -jnp.inf)
            from_state = cumf + m_prev
            m_row = jnp.maximum(jnp.max(dmat, axis=1, keepdims=True), from_state)
            w = jnp.exp(dmat - m_row)
            s_state = jnp.exp(from_state - m_row)
            qk = _dot(qh, kh, dims=((1,), (1,))) * w
            num = _dot(qk, vh) + _dot(qh, cmat) * s_state
            den = jnp.sum(qk, axis=1, keepdims=True) + jnp.sum(qh * nvec, axis=1, keepdims=True) * s_state
            den = jnp.maximum(jnp.abs(den), jnp.exp(-m_row))
            hout = num / den
            m_new = m_row[q - 1:q, :]
            cumf_last = cumf[q - 1:q, :]
            w_end = jnp.exp(cumf_last - cumf + log_i - m_new)
            s_end = jnp.exp(cumf_last + m_prev - m_new)
            kw = kh * w_end
            c_ref[bi, hi] = s_end * cmat + _dot(kw, vh, dims=((0,), (0,)))
            n_ref[bi, hi] = s_end * nvec + jnp.sum(kw, axis=0, keepdims=True)
            m_ref[bi, hi] = m_new
            gate_o = jax.nn.sigmoid(o_ref[r0:r0 + q, hi * dv:(hi + 1) * dv])
            y_ref[r0:r0 + q, hi * dv:(hi + 1) * dv] = (
                _rms(hout, gn_ref[:, hi * dv:(hi + 1) * dv]) * gate_o).astype(y_ref.dtype)


def _mlstm(qk_c, xw_b, v_col0, o_col0, gt_h, gb_h, gnorm, state0, mix, y_col0, row0, nb, seq, q, heads, dk, dv,
           blocking, name):
    nc = seq // q
    bb, hb = _scan_blocking(nb, heads, nc, *blocking, col_offsets=(v_col0, o_col0, y_col0), unit_width=dv)
    rq = bb * q
    assert row0 % rq == 0 and v_col0 % (hb * dv) == 0 and o_col0 % (hb * dv) == 0 and y_col0 % (hb * dv) == 0
    rb0 = row0 // rq
    vb0, ob0, yb0 = v_col0 // (hb * dv), o_col0 // (hb * dv), y_col0 // (hb * dv)
    kb0 = heads // hb
    zero_init = state0 is None
    in_specs = [
        pl.BlockSpec((rq, hb * dk), lambda b, h, c: (b * nc + c, h)),
        pl.BlockSpec((rq, hb * dk), lambda b, h, c: (b * nc + c, kb0 + h)),
        pl.BlockSpec((rq, hb * dv), lambda b, h, c: (rb0 + b * nc + c, vb0 + h)),
        pl.BlockSpec((rq, hb * dv), lambda b, h, c: (rb0 + b * nc + c, ob0 + h)),
        pl.BlockSpec((hb, rq, 2), lambda b, h, c: (h, rb0 + b * nc + c, 0)),
        pl.BlockSpec((hb, 1, 2), lambda b, h, c: (h, 0, 0)),
        pl.BlockSpec((1, hb * dv), lambda b, h, c: (0, h)),
    ]
    args = [qk_c, qk_c, xw_b, xw_b, gt_h, gb_h, gnorm]
    st_specs = (pl.BlockSpec((bb, hb, dk, dv), lambda b, h, c: (b, h, 0, 0)),
                pl.BlockSpec((bb, hb, 1, dk), lambda b, h, c: (b, h, 0, 0)),
                pl.BlockSpec((bb, hb, 1, 1), lambda b, h, c: (b, h, 0, 0)))
    if not zero_init:
        in_specs.extend(st_specs)
        args.extend(state0)
    in_specs.append(pl.BlockSpec(memory_space=pl.ANY))
    args.append(mix)
    return pl.pallas_call(
        functools.partial(_mlstm_body, q=q, dk=dk, dv=dv, bb=bb, hb=hb, zero_init=zero_init, carry=nc > 1),
        out_shape=(jax.ShapeDtypeStruct(mix.shape, mix.dtype),
                   jax.ShapeDtypeStruct((nb, heads, dk, dv), F32),
                   jax.ShapeDtypeStruct((nb, heads, 1, dk), F32),
                   jax.ShapeDtypeStruct((nb, heads, 1, 1), F32)),
        grid=(nb // bb, heads // hb, nc),
        in_specs=in_specs,
        out_specs=(pl.BlockSpec((rq, hb * dv), lambda b, h, c: (rb0 + b * nc + c, yb0 + h)),) + st_specs,
        input_output_aliases={len(args) - 1: 0},
        compiler_params=_cparams(3),
        name=name,
    )(*args)


def _gla_body(*refs, q, dk, dv, bb, hb, zero_init, carry):
    refs = list(refs)
    q_ref, k_ref, v_ref, r_ref, ga_ref, wgu_ref, bg_ref, gn_ref = refs[:8]
    pos = 8
    s0_ref = None
    if not zero_init:
        s0_ref = refs[pos]
        pos += 1
    y_ref, s_ref = refs[pos + 1:pos + 3]
    _init_carried(carry, zero_init, [(s_ref, s0_ref)])
    tril = _tril(q)
    trilf = tril.astype(F32)
    ones = jnp.ones((q, LANE), F32)
    for bi in range(bb):
        r0 = bi * q
        ga = ga_ref[r0:r0 + q, :]
        for hi in range(hb):
            kh = k_ref[r0:r0 + q, hi * dk:(hi + 1) * dk]
            vh = v_ref[r0:r0 + q, hi * dv:(hi + 1) * dv]
            gate_pre = _dot(ga, wgu_ref[:, hi * dk:(hi + 1) * dk]) + bg_ref[:, hi * dk:(hi + 1) * dk]
            log_a = jax.nn.log_sigmoid(gate_pre) / GLA_TAU
            bcum = _dot_exact_rhs(trilf, log_a)
            qt = q_ref[r0:r0 + q, hi * dk:(hi + 1) * dk] * (dk ** -0.5) * jnp.exp(bcum)
            kt = kh * jnp.exp(-bcum)
            att = jnp.where(tril, _dot(qt, kt, dims=((1,), (1,))), 0.0)
            smat = _read_state(carry, zero_init, s_ref, s0_ref, bi, hi)
            o = _dot(att, vh) + _dot(qt, smat)
            b_last = bcum[q - 1:q, :]
            kw = kh * jnp.exp(b_last - bcum)
            tot = _dot_exact_lhs(log_a, ones, dims=((0,), (0,)))
            decay = jnp.exp(tot[:, 0:1])
            s_ref[bi, hi] = decay * smat + _dot(kw, vh, dims=((0,), (0,)))
            gate_r = _silu(r_ref[r0:r0 + q, hi * dv:(hi + 1) * dv])
            y_ref[r0:r0 + q, hi * dv:(hi + 1) * dv] = (
                _rms(o, gn_ref[:, hi * dv:(hi + 1) * dv]) * gate_r).astype(y_ref.dtype)


def _gla(xw, q_col0, k_col0, v_col0, r_col0, small, wgu, bg, gnorm, s0, mix, row0, nb, seq, q, heads, dk, dv,
         blocking, name):
    nc = seq // q
    bb, hb = _scan_blocking(nb, heads, nc, *blocking)
    rq = bb * q
    kw_, vw_ = hb * dk, hb * dv
    assert row0 % rq == 0 and q_col0 % kw_ == 0 and k_col0 % kw_ == 0 and v_col0 % vw_ == 0 and r_col0 % vw_ == 0
    rb0 = row0 // rq
    qb0, kb0, vb0, rcb0 = q_col0 // kw_, k_col0 // kw_, v_col0 // vw_, r_col0 // vw_
    zero_init = s0 is None
    in_specs = [
        pl.BlockSpec((rq, kw_), lambda b, h, c: (rb0 + b * nc + c, qb0 + h)),
        pl.BlockSpec((rq, kw_), lambda b, h, c: (rb0 + b * nc + c, kb0 + h)),
        pl.BlockSpec((rq, vw_), lambda b, h, c: (rb0 + b * nc + c, vb0 + h)),
        pl.BlockSpec((rq, vw_), lambda b, h, c: (rb0 + b * nc + c, rcb0 + h)),
        pl.BlockSpec((rq, LANE), lambda b, h, c: (rb0 + b * nc + c, 0)),
        pl.BlockSpec((LANE, kw_), lambda b, h, c: (0, h)),
        pl.BlockSpec((1, kw_), lambda b, h, c: (0, h)),
        pl.BlockSpec((1, vw_), lambda b, h, c: (0, h)),
    ]
    args = [xw, xw, xw, xw, small, wgu, bg, gnorm]
    st_spec = pl.BlockSpec((bb, hb, dk, dv), lambda b, h, c: (b, h, 0, 0))
    if not zero_init:
        in_specs.append(st_spec)
        args.append(s0)
    in_specs.append(pl.BlockSpec(memory_space=pl.ANY))
    args.append(mix)
    return pl.pallas_call(
        functools.partial(_gla_body, q=q, dk=dk, dv=dv, bb=bb, hb=hb, zero_init=zero_init, carry=nc > 1),
        out_shape=(jax.ShapeDtypeStruct(mix.shape, mix.dtype),
                   jax.ShapeDtypeStruct((nb, heads, dk, dv), F32)),
        grid=(nb // bb, heads // hb, nc),
        in_specs=in_specs,
        out_specs=(pl.BlockSpec((rq, vw_), lambda b, h, c: (rb0 + b * nc + c, h)), st_spec),
        input_output_aliases={len(args) - 1: 0},
        compiler_params=_cparams(3),
        name=name,
    )(*args)


def _router_body(x_ref, g_ref, wr_ref, idx_ref, gate_ref, *, n_experts):
    xn = _rms(x_ref[...], g_ref[...])
    wr = wr_ref[...]
    xh = xn.astype(BF16)
    xl = (xn - xh.astype(F32)).astype(BF16)
    wh = wr.astype(BF16)
    wl = (wr - wh.astype(F32)).astype(BF16)
    logits = (jnp.dot(xh, wh, preferred_element_type=F32) + jnp.dot(xh, wl, preferred_element_type=F32)
              + jnp.dot(xl, wh, preferred_element_type=F32))
    lane = _iota2(logits.shape, 1)
    neg = -jnp.inf
    l1 = jnp.where(lane < n_experts, logits, neg)
    m1 = jnp.max(l1, axis=1, keepdims=True)
    i1 = jnp.min(jnp.where(l1 == m1, lane, N_EXPERTS_PAD), axis=1, keepdims=True)
    l2 = jnp.where(lane == i1, neg, l1)
    m2 = jnp.max(l2, axis=1, keepdims=True)
    i2 = jnp.min(jnp.where(l2 == m2, lane, N_EXPERTS_PAD), axis=1, keepdims=True)
    e2 = jnp.exp(m2 - m1)
    denom = 1.0 + e2
    idx_ref[...] = jnp.where(lane == 0, i1, jnp.where(lane == 1, i2, 0))
    gate_ref[...] = jnp.where(lane == 0, 1.0 / denom, jnp.where(lane == 1, e2 / denom, 0.0))


def _router(x, g, w_router):
    r, d = x.shape
    n_experts = w_router.shape[1]
    wr = jnp.pad(w_router, ((0, 0), (0, N_EXPERTS_PAD - n_experts)))
    tr = _row_tile(r, 256)
    return pl.pallas_call(
        functools.partial(_router_body, n_experts=n_experts),
        out_shape=(jax.ShapeDtypeStruct((r, N_EXPERTS_PAD), jnp.int32),
                   jax.ShapeDtypeStruct((r, N_EXPERTS_PAD), F32)),
        grid=(r // tr,),
        in_specs=[pl.BlockSpec((tr, d), lambda i: (i, 0)), pl.BlockSpec((1, d), lambda i: (0, 0)),
                  pl.BlockSpec((d, N_EXPERTS_PAD), lambda i: (0, 0))],
        out_specs=(pl.BlockSpec((tr, N_EXPERTS_PAD), lambda i: (i, 0)),
                   pl.BlockSpec((tr, N_EXPERTS_PAD), lambda i: (i, 0))),
        compiler_params=_cparams(1),
        name="moe_router",
    )(x, g.reshape(1, d), wr)


def _row_copy(src_hbm, row, dst_vmem, slot, sem):
    return pltpu.make_async_copy(src_hbm.at[pl.ds(row, 1)], dst_vmem.at[pl.ds(slot, 1)], sem)


DMA_LOOP_UNROLL = 8


def _two_slot_gather(step, n_steps, issue_rows, wait_rows):
    slot = step % 2

    @pl.when(step == 0)
    def _prime():
        issue_rows(0, 0)

    @pl.when(step + 1 < n_steps)
    def _ahead():
        issue_rows(step + 1, 1 - slot)

    wait_rows(slot)
    return slot


def _gather_norm_body(src_ref, used_ref, x_hbm, g_ref, o_ref, buf_ref, sem, *, tg, n_steps):
    me = pl.program_id(0)

    def issue_rows(step, slot):
        base = step * tg

        def body(r, carry):
            _row_copy(x_hbm, src_ref[base + r], buf_ref.at[slot], r, sem.at[slot]).start()
            return carry

        @pl.when(used_ref[step] == 1)
        def _():
            lax.fori_loop(0, tg, body, 0, unroll=DMA_LOOP_UNROLL)

    def wait_rows(slot):
        def body(r, carry):
            _row_copy(x_hbm, 0, buf_ref.at[slot], r, sem.at[slot]).wait()
            return carry

        @pl.when(used_ref[me] == 1)
        def _():
            lax.fori_loop(0, tg, body, 0, unroll=DMA_LOOP_UNROLL)

    slot = _two_slot_gather(me, n_steps, issue_rows, wait_rows)

    @pl.when(used_ref[me] == 1)
    def _norm():
        o_ref[...] = _rms(buf_ref[slot], g_ref[...]).astype(o_ref.dtype)

    @pl.when(used_ref[me] == 0)
    def _skip():
        o_ref[...] = jnp.zeros_like(o_ref)


def _gather_norm(x, g, src_rows, used, tg):
    _, d = x.shape
    p_rows = src_rows.shape[0]
    n_steps = p_rows // tg
    return pl.pallas_call(
        functools.partial(_gather_norm_body, tg=tg, n_steps=n_steps),
        out_shape=jax.ShapeDtypeStruct((p_rows, d), BF16),
        grid_spec=pltpu.PrefetchScalarGridSpec(
            num_scalar_prefetch=2,
            grid=(n_steps,),
            in_specs=[pl.BlockSpec(memory_space=pl.ANY), pl.BlockSpec((1, d), lambda i, s, u: (0, 0))],
            out_specs=pl.BlockSpec((tg, d), lambda i, s, u: (i, 0)),
            scratch_shapes=[pltpu.VMEM((2, tg, d), F32), pltpu.SemaphoreType.DMA((2,))],
        ),
        compiler_params=_cparams(1),
        name="moe_gather_norm",
    )(src_rows, used, x, g.reshape(1, d))


def _combine_body(pos_ref, x_ref, y_hbm, g_ref, oa_ref, ob_ref, buf_ref, sem, *, tc, top_k, na_tiles, n_steps):
    step = pl.program_id(0)

    def issue_rows(s, slot):
        base = s * tc

        def body(r, carry):
            for k in range(top_k):
                _row_copy(y_hbm, pos_ref[(base + r) * top_k + k], buf_ref.at[slot, k], r, sem.at[slot]).start()
            return carry

        lax.fori_loop(0, tc, body, 0, unroll=DMA_LOOP_UNROLL)

    def wait_rows(slot):
        def body(r, carry):
            for k in range(top_k):
                _row_copy(y_hbm, 0, buf_ref.at[slot, k], r, sem.at[slot]).wait()
            return carry

        lax.fori_loop(0, tc, body, 0, unroll=DMA_LOOP_UNROLL)

    slot = _two_slot_gather(step, n_steps, issue_rows, wait_rows)
    y = buf_ref[slot, 0]
    for k in range(1, top_k):
        y = y + buf_ref[slot, k]
    out = _rms(x_ref[...] + y, g_ref[...])

    @pl.when(step < na_tiles)
    def _first():
        oa_ref[...] = out

    @pl.when(step >= na_tiles)
    def _second():
        ob_ref[...] = out


def _combine_norm(x, y_sorted, pos, g, top_k, n_a, n_b):
    _, d = x.shape
    tc = _row_tile(math.gcd(n_a, n_b), 256)
    na_tiles, nb_tiles = n_a // tc, n_b // tc
    return pl.pallas_call(
        functools.partial(_combine_body, tc=tc, top_k=top_k, na_tiles=na_tiles, n_steps=na_tiles + nb_tiles),
        out_shape=(jax.ShapeDtypeStruct((n_a, d), F32), jax.ShapeDtypeStruct((n_b, d), F32)),
        grid_spec=pltpu.PrefetchScalarGridSpec(
            num_scalar_prefetch=1,
            grid=(na_tiles + nb_tiles,),
            in_specs=[pl.BlockSpec((tc, d), lambda i, s: (i, 0)), pl.BlockSpec(memory_space=pl.ANY),
                      pl.BlockSpec((1, d), lambda i, s: (0, 0))],
            out_specs=(pl.BlockSpec((tc, d), lambda i, s: (jnp.minimum(i, na_tiles - 1), 0)),
                       pl.BlockSpec((tc, d), lambda i, s: (jnp.maximum(i - na_tiles, 0), 0))),
            scratch_shapes=[pltpu.VMEM((2, top_k, tc, d), F32), pltpu.SemaphoreType.DMA((2,))],
        ),
        compiler_params=_cparams(1),
        name="moe_combine_norm",
    )(pos, x, y_sorted, g.reshape(1, d))


MOE_ROW_TILE_MAX = 1024
MOE_SUB_ROWS = 128


def _moe_row_tile(n_assign, n_experts):
    tm = 2 * SUBLANE
    while tm * 2 <= min(MOE_ROW_TILE_MAX, n_assign // (2 * n_experts)):
        tm *= 2
    return tm


def _moe_plan(top_idx, top_gate, n_experts, tm, sub):
    r, top_k = top_idx.shape
    n_assign = r * top_k
    n_tiles = (n_assign + n_experts * (tm - 1)) // tm
    flat_e = top_idx.reshape(-1)
    onehot = (flat_e[:, None] == jnp.arange(n_experts, dtype=jnp.int32)[None, :]).astype(jnp.int32)
    csum = jnp.cumsum(onehot, axis=0)
    counts = csum[-1]
    tiles_per = (counts + tm - 1) // tm
    tile_end = jnp.cumsum(tiles_per)
    tile_start = tile_end - tiles_per
    count_start = jnp.cumsum(counts) - counts
    rank = jnp.sum((csum - onehot) * onehot, axis=1)
    pos = jnp.sum(onehot * (tile_start * tm)[None, :], axis=1) + rank
    total = tile_end[-1]
    t = jnp.arange(n_tiles, dtype=jnp.int32)
    valid = t < total
    t_eff = jnp.minimum(t, total - 1)
    te = jnp.sum((t_eff[:, None] >= tile_end[None, :]).astype(jnp.int32), axis=1)
    te = jnp.minimum(te, n_experts - 1)
    first = valid & (t == tile_start[te])
    occupied = jnp.clip(counts[te] - (t - tile_start[te]) * tm, 0, tm)
    occupied = jnp.where(valid, (occupied + sub - 1) // sub * sub, 0).astype(jnp.int32)
    tiles = (te, first.astype(jnp.int32), occupied, t)
    order = jnp.argsort(flat_e, stable=True).astype(jnp.int32)
    slot = jnp.arange(n_tiles * tm, dtype=jnp.int32)
    slot_e = jnp.repeat(te, tm)
    local = slot - jnp.repeat(tile_start[te] * tm, tm)
    filled = jnp.repeat(valid, tm) & (local < counts[slot_e])
    assign = order[jnp.clip(count_start[slot_e] + local, 0, n_assign - 1)]
    src_rows = jnp.where(filled, assign // top_k, 0)
    row_gate = jnp.where(filled, top_gate.reshape(-1)[assign], 0.0)
    return tiles, pos, src_rows, row_gate.reshape(-1, 1)


def _prev_rows(rows3):
    return jnp.pad(rows3, ((0, 0), (SUBLANE - (CONV_W - 1), 0), (0, 0)))


def kernel(x_prompt, x_sample, state_ssd_conv, state_ssd, state_mlstm_conv, state_mlstm_c, state_mlstm_n,
           state_mlstm_m, state_gla, meta_tokens,
           norm_mix_even, w_in_even, ssd_conv_w, ssd_conv_b, ssd_dt_bias, ssd_a_log, ssd_d, ssd_norm,
           mlstm_conv_w, mlstm_conv_b, mlstm_if_bias, mlstm_norm, w_out_even,
           norm_ffn_even, ffn_w_gate, ffn_w_up, ffn_w_down,
           norm_mix_odd, w_in_odd, gla_w_gate_up, gla_b_gate, gla_norm, w_out_odd,
           norm_ffn_odd, moe_router, moe_w1, moe_w3, moe_w2, final_norm):
    bp, sp, d = x_prompt.shape
    bs, ss, _ = x_sample.shape
    n_meta = meta_tokens.shape[0]
    assert w_in_even.shape[0] == 1 and w_in_odd.shape[0] == 1, "one even and one odd layer"

    ssd_heads = ssd_a_log.shape[-1]
    d_inner = ssd_norm.shape[-1]
    p_dim = d_inner // ssd_heads
    conv_dim = ssd_conv_w.shape[-1]
    n_state = (conv_dim - d_inner) // (2 * SSD_GROUPS)
    hpg = ssd_heads // SSD_GROUPS
    ml_heads = mlstm_if_bias.shape[-1] // 2
    ml_v = mlstm_norm.shape[-1]
    ml_dv = ml_v // ml_heads
    ml_qk = mlstm_conv_w.shape[-1] // 2
    ml_dk = ml_qk // ml_heads
    gla_heads, gla_dk, gla_dv = state_gla.shape[2:]
    gla_rank = gla_w_gate_up.shape[1]
    n_experts = moe_router.shape[-1]
    top_k = 2

    n_p, n_s, n_m = bp * sp, bs * ss, bp * n_meta
    rows = n_p + n_s + n_m
    row_p, row_s, row_m = 0, n_p, n_p + n_s
    seg_m = (row_m, bp, n_meta, (n_meta,) * 3)
    seg_p = (row_p, bp, sp, PROMPT_CHUNKS)
    seg_s = (row_s, bs, ss, (ss,) * 3)

    meta_rows = jnp.broadcast_to(meta_tokens.astype(F32)[None], (bp, n_meta, d)).reshape(n_m, d)
    x0 = jnp.concatenate([x_prompt.reshape(n_p, d), x_sample.reshape(n_s, d), meta_rows], axis=0)

    xn = _rmsnorm(x0, norm_mix_even[0])
    w_in = w_in_even
    off_z, off_xbc = 0, d_inner
    off_dt = off_xbc + conv_dim
    off_qk = off_dt + ssd_heads
    off_v = off_qk + 2 * ml_qk
    off_if = off_v + ml_v
    off_o = off_if + 2 * ml_heads
    wt_in = jnp.swapaxes(w_in[0], 0, 1)
    tn_in = _pick(math.gcd(math.gcd(off_dt, 2 * ml_qk + ml_v), ml_v), (512, 256, 128))
    slabs = ([o for o in range(0, off_dt, tn_in)] + [o for o in range(off_qk, off_if, tn_in)]
             + [o for o in range(off_o, off_o + ml_v, tn_in)])
    xw = _dense_mm_nt(xn, wt_in, slabs, tn_in, "in_even")
    col_qk = off_dt
    col_v = col_qk + 2 * ml_qk
    col_o = col_v + ml_v
    n_small = ssd_heads + 2 * ml_heads
    wt_s = jnp.concatenate([wt_in[off_dt:off_qk], wt_in[off_if:off_o]], axis=0)
    wt_s = jnp.pad(wt_s, ((0, LANE - n_small), (0, 0)))
    small = _dense_mm_nt(xn, wt_s, [0], LANE, "in_even_small")

    dt_g = small[:, :ssd_heads].reshape(rows, SSD_GROUPS, hpg).transpose(1, 0, 2)
    par_g = jnp.stack([ssd_dt_bias[0], ssd_a_log[0], ssd_d[0]], axis=0).reshape(3, SSD_GROUPS, hpg)
    par_g = par_g.transpose(1, 0, 2)
    gt_h = small[:, ssd_heads:n_small].reshape(rows, 2, ml_heads).transpose(2, 0, 1)
    gb_h = mlstm_if_bias[0].reshape(2, ml_heads).transpose(1, 0).reshape(ml_heads, 1, 2)
    ssd_gn = ssd_norm[0].reshape(1, d_inner)
    ml_gn = mlstm_norm[0].reshape(1, ml_v)

    def even_segment(seg, prev_ssd, prev_ml, st, mix):
        row0, nb, seq, qs = seg
        xbc_c, tail_ssd = _conv_silu(xw, off_xbc, conv_dim, row0, nb, seq, prev_ssd,
                                     ssd_conv_w[0], ssd_conv_b[0], "ssd_conv")
        qk_c, tail_ml = _conv_silu(xw, col_qk, 2 * ml_qk, row0, nb, seq, prev_ml,
                                   mlstm_conv_w[0], mlstm_conv_b[0], "mlstm_conv")
        h0 = None if st is None else st[0]
        mix, h_new = _ssd(xbc_c, xw, off_z, dt_g, par_g, ssd_gn, h0, mix, 0, row0, nb, seq,
                          math.gcd(seq, qs[0]), hpg, p_dim, n_state, SSD_BLOCKING, "ssd_scan")
        ml0 = None if st is None else st[1:]
        mix, c_new, n_new, m_new = _mlstm(qk_c, xw, col_v, col_o, gt_h, gb_h, ml_gn, ml0,
                                          mix, d_inner, row0, nb, seq, math.gcd(seq, qs[1]), ml_heads, ml_dk,
                                          ml_dv, MLSTM_BLOCKING, "mlstm_scan")
        return mix, (h_new, c_new, n_new, m_new), tail_ssd, tail_ml

    zeros_ssd = jnp.zeros((bp, SUBLANE, conv_dim), F32)
    zeros_ml = jnp.zeros((bp, SUBLANE, 2 * ml_qk), F32)
    mix = jnp.zeros((rows, d_inner + ml_v), BF16)
    mix, st_m, cs_m, cm_m = even_segment(seg_m, zeros_ssd, zeros_ml, None, mix)
    mix, st_p, cs_p, cm_p = even_segment(seg_p, cs_m, cm_m, st_m, mix)
    st_s0 = (state_ssd[0].reshape(bs, SSD_GROUPS, hpg * p_dim, n_state), state_mlstm_c[0],
             state_mlstm_n[0].reshape(bs, ml_heads, 1, ml_dk), state_mlstm_m[0].reshape(bs, ml_heads, 1, 1))
    mix, st_s, cs_s, cm_s = even_segment(seg_s, _prev_rows(state_ssd_conv[0]), _prev_rows(state_mlstm_conv[0]),
                                         st_s0, mix)

    x1 = _dense_mm(mix, w_out_even, res=x0, name="out_even")
    xn = _rmsnorm(x1, norm_ffn_even[0])
    hid = _dense_swiglu(xn, ffn_w_gate, ffn_w_up, "ffn_up")
    x2 = _dense_mm(hid, ffn_w_down, res=x1, name="ffn_down")

    xn = _rmsnorm(x2, norm_mix_odd[0])
    n_main = 2 * gla_heads * gla_dk + 2 * gla_heads * gla_dv
    wt_odd = jnp.swapaxes(w_in_odd[0], 0, 1)
    tn_odd = _pick(n_main, (512, 256, 128))
    xw_o = _dense_mm_nt(xn, wt_odd, list(range(0, n_main, tn_odd)), tn_odd, "in_odd")
    wt_ga = jnp.pad(wt_odd[n_main:n_main + gla_rank], ((0, LANE - gla_rank), (0, 0)))
    small_o = _dense_mm_nt(xn, wt_ga, [0], LANE, "in_odd_small")
    gla_gn = gla_norm[0].reshape(1, gla_heads * gla_dv)
    bg = gla_b_gate[0].reshape(1, gla_heads * gla_dk)
    wgu_pad = jnp.pad(gla_w_gate_up[0], ((0, LANE - gla_rank), (0, 0)))
    k_col0 = gla_heads * gla_dk
    v_col0 = 2 * gla_heads * gla_dk
    r_col0 = v_col0 + gla_heads * gla_dv

    def odd_segment(seg, s0, mix):
        row0, nb, seq, qs = seg
        return _gla(xw_o, 0, k_col0, v_col0, r_col0, small_o, wgu_pad, bg, gla_gn, s0, mix, row0, nb, seq,
                    math.gcd(seq, qs[2]), gla_heads, gla_dk, gla_dv, GLA_BLOCKING, "gla_scan")

    mix = jnp.zeros((rows, gla_heads * gla_dv), BF16)
    mix, s_m = odd_segment(seg_m, None, mix)
    mix, s_p = odd_segment(seg_p, s_m, mix)
    mix, s_s = odd_segment(seg_s, state_gla[0], mix)
    x3 = _dense_mm(mix, w_out_odd, res=x2, name="out_odd")

    idx_pad, gate_pad = _router(x3, norm_ffn_odd[0], moe_router[0])
    tm_e = _moe_row_tile(rows * top_k, n_experts)
    sub_e = max(2 * SUBLANE, min(MOE_SUB_ROWS, tm_e // 2))
    tiles_e, pos, src_rows, row_gate = _moe_plan(idx_pad[:, :top_k], gate_pad[:, :top_k], n_experts, tm_e, sub_e)
    tg = _pick(sub_e, (256, 128, 64, 32, 16, 8))
    g_start = jnp.arange(src_rows.shape[0] // tg, dtype=jnp.int32) * tg
    g_used = (g_start % tm_e < tiles_e[2][g_start // tm_e]).astype(jnp.int32)
    xs = _gather_norm(x3, norm_ffn_odd[0], src_rows, g_used, tg)
    _, tn_up = _plan_swiglu(d, moe_w1.shape[-1], (tm_e,))
    hs = _gmm_swiglu(xs, moe_w1[0], moe_w3[0], tiles_e, tm=tm_e, tn=tn_up, sub=sub_e, name="moe_up")
    _, tn_dn, tk_dn = _plan_mm(moe_w2.shape[-2], d, 0, (tm_e,), False)
    ys = _gmm(hs, moe_w2[0], tiles_e, tm=tm_e, tn=tn_dn, tk=tk_dn, sub=sub_e, scale=row_gate, name="moe_down")
    y_prompt, y_sample = _combine_norm(x3, ys, pos, final_norm, top_k, n_p, n_s)
    y_prompt = y_prompt.reshape(bp, sp, d)
    y_sample = y_sample.reshape(bs, ss, d)

    def pack_even(cs, cm, st, nb):
        h_new, c_new, n_new, m_new = st
        keep = SUBLANE - (CONV_W - 1)
        return (cs[None, :, keep:, :], h_new.reshape(1, nb, ssd_heads, p_dim, n_state), cm[None, :, keep:, :],
                c_new[None],
                n_new.reshape(1, nb, ml_heads, ml_dk), m_new.reshape(1, nb, ml_heads))

    return ((y_prompt, y_sample) + pack_even(cs_p, cm_p, st_p, bp) + (s_p[None],)
            + pack_even(cs_s, cm_s, st_s, bs) + (s_s[None],))
```
